```python
import math
import jax, jax.numpy as jnp
from jax import lax
import numpy as np

D_MODEL = 2048
BATCH = 8
SEQ = 4096
DEPTH = 1

N_META = 16
CHUNK = 64
NORM_EPS = 1e-6
DN_HEADS = D_MODEL // 128
DN_DK = 128
DN_DV = 128
DN_W = DN_HEADS * DN_DV
DN_CONV = 4
M2_HEAD_DIM = 64
M2_HEADS = D_MODEL // M2_HEAD_DIM
M2_GROUPS = 4
M2_HPG = M2_HEADS // M2_GROUPS
M2_STATE = 128
M2_W = M2_HEADS * M2_HEAD_DIM
M2_CONV = 4
D_MIX = DN_W + M2_W
D_IN_PROJ = 4 * DN_W + 2 * DN_HEADS + 2 * M2_W + 2 * M2_GROUPS * M2_STATE + M2_HEADS
D_FF = 11 * D_MODEL // 4
FFN_CONV = 3

kernel_name = "hybrid_gdn_mamba2_convffn_meta"


def rms_norm(x, w):
    xf = x.astype(jnp.float32)
    y = xf * lax.rsqrt(jnp.mean(xf * xf, axis=-1, keepdims=True) + NORM_EPS)
    return (y * w.astype(jnp.float32)).astype(x.dtype)


def causal_dwconv(x, w):
    k, c = w.shape
    return lax.conv_general_dilated(x, w[:, None, :].astype(x.dtype), window_strides=(1,),
                                    padding=[(k - 1, 0)],
                                    dimension_numbers=('NWC', 'WIO', 'NWC'),
                                    feature_group_count=c)


def to_chunks(t):
    pad = [(0, 0), (CHUNK - N_META, 0)] + [(0, 0)] * (t.ndim - 2)
    t = jnp.pad(t, pad)
    b, lp = t.shape[:2]
    t = t.reshape((b, lp // CHUNK, CHUNK) + t.shape[2:])
    return jnp.moveaxis(t, 1, 0)


def from_chunks(t):
    t = jnp.moveaxis(t, 0, 1)
    t = t.reshape((t.shape[0], t.shape[1] * t.shape[2]) + t.shape[3:])
    return t[:, CHUNK - N_META:]


def gated_delta_rule(q, k, v, beta, g):
    causal = jnp.tril(jnp.ones((CHUNK, CHUNK), dtype=bool))
    strict = jnp.tril(jnp.ones((CHUNK, CHUNK), dtype=bool), -1)

    def step(state, inp):
        qc, kc, vc, bc, gc = inp
        gcum = jnp.cumsum(gc, axis=1)
        gh = jnp.swapaxes(gcum, 1, 2)
        decay = jnp.exp(jnp.where(causal, gh[..., :, None] - gh[..., None, :], -jnp.inf))
        kk = jnp.einsum('blhd,bshd->bhls', kc, kc)
        bh = jnp.swapaxes(bc, 1, 2)
        a_mat = jnp.where(strict, bh[..., :, None] * kk * decay, 0.0)
        rhs = jnp.concatenate([vc * bc[..., None], kc * (bc * jnp.exp(gcum))[..., None]], axis=-1)
        rhs = jnp.swapaxes(rhs, 1, 2)
        sol = lax.linalg.triangular_solve(a_mat, rhs, left_side=True, lower=True,
                                          unit_diagonal=True)
        u, w = sol[..., :DN_DV], sol[..., DN_DV:]
        v_new = u - jnp.einsum('bhlk,bhkv->bhlv', w, state)
        o_inter = jnp.einsum('blhk,bhkv->bhlv', qc * jnp.exp(gcum)[..., None], state)
        qk = jnp.einsum('blhd,bshd->bhls', qc, kc) * decay
        o = o_inter + jnp.einsum('bhls,bhsv->bhlv', qk, v_new)
        g_last = gcum[:, -1:, :]
        state = (state * jnp.exp(g_last[:, 0])[..., None, None]
                 + jnp.einsum('bshk,bhsv->bhkv', kc * jnp.exp(g_last - gcum)[..., None], v_new))
        return state, jnp.swapaxes(o, 1, 2)

    b = q.shape[0]
    s0 = jnp.zeros((b, DN_HEADS, DN_DK, DN_DV), jnp.float32)
    _, ys = lax.scan(step, s0, (to_chunks(q), to_chunks(k), to_chunks(v),
                                to_chunks(beta), to_chunks(g)))
    return from_chunks(ys)


def ssd_chunked(xs, dt, a, bm, cm):
    causal = jnp.tril(jnp.ones((CHUNK, CHUNK), dtype=bool))[None, :, :, None, None]

    def step(state, inp):
        xc, dtc, ac, bc, cc = inp
        acs = jnp.cumsum(ac, axis=1)
        lmat = jnp.exp(jnp.where(causal, acs[:, :, None] - acs[:, None, :], -jnp.inf))
        xdt = xc * dtc[..., None]
        cb = jnp.einsum('blgn,bsgn->blsg', cc, bc)
        y_diag = jnp.einsum('blsg,blsgr,bsgrp->blgrp', cb, lmat, xdt)
        y_off = jnp.einsum('blgn,bgrpn->blgrp', cc, state) * jnp.exp(acs)[..., None]
        a_last = acs[:, -1]
        state = (state * jnp.exp(a_last)[..., None, None]
                 + jnp.einsum('bsgn,bsgr,bsgrp->bgrpn', bc, jnp.exp(a_last[:, None] - acs), xdt))
        return state, y_diag + y_off

    b = xs.shape[0]
    s0 = jnp.zeros((b, M2_GROUPS, M2_HPG, M2_HEAD_DIM, M2_STATE), jnp.float32)
    _, ys = lax.scan(step, s0, (to_chunks(xs), to_chunks(dt), to_chunks(a),
                                to_chunks(bm), to_chunks(cm)))
    return from_chunks(ys)


def hybrid_mixer(h, w_in, dn_conv_w, dn_a_log, dn_dt_bias, dn_norm_w,
                 m2_conv_w, m2_conv_b, m2_a_log, m2_dt_bias, m2_d, m2_norm_w, w_out):
    b, l, _ = h.shape
    f32 = jnp.float32
    proj = h @ w_in
    sizes = [3 * DN_W, DN_W, DN_HEADS, DN_HEADS, M2_W, M2_W + 2 * M2_GROUPS * M2_STATE]
    dn_qkv, dn_z, dn_b, dn_a, m2_z, m2_xbc, m2_dt = jnp.split(proj, list(np.cumsum(sizes)), axis=-1)

    qkv = jax.nn.silu(causal_dwconv(dn_qkv, dn_conv_w)).astype(f32)
    q, k, v = jnp.split(qkv, 3, axis=-1)
    q = q.reshape(b, l, DN_HEADS, DN_DK)
    k = k.reshape(b, l, DN_HEADS, DN_DK)
    v = v.reshape(b, l, DN_HEADS, DN_DV)
    q = q * lax.rsqrt(jnp.sum(q * q, -1, keepdims=True) + NORM_EPS) * (DN_DK ** -0.5)
    k = k * lax.rsqrt(jnp.sum(k * k, -1, keepdims=True) + NORM_EPS)
    beta = jax.nn.sigmoid(dn_b.astype(f32))
    g = -jnp.exp(dn_a_log.astype(f32)) * jax.nn.softplus(dn_a.astype(f32) + dn_dt_bias.astype(f32))
    o = gated_delta_rule(q, k, v, beta, g)
    z = dn_z.astype(f32).reshape(b, l, DN_HEADS, DN_DV)
    o = (o * lax.rsqrt(jnp.mean(o * o, -1, keepdims=True) + NORM_EPS)
         * dn_norm_w.astype(f32) * jax.nn.silu(z)).reshape(b, l, DN_W)

    xbc = jax.nn.silu(causal_dwconv(m2_xbc, m2_conv_w) + m2_conv_b.astype(h.dtype)).astype(f32)
    xs, bm, cm = jnp.split(xbc, [M2_W, M2_W + M2_GROUPS * M2_STATE], axis=-1)
    xs = xs.reshape(b, l, M2_GROUPS, M2_HPG, M2_HEAD_DIM)
    bm = bm.reshape(b, l, M2_GROUPS, M2_STATE)
    cm = cm.reshape(b, l, M2_GROUPS, M2_STATE)
    dt = jax.nn.softplus(m2_dt.astype(f32) + m2_dt_bias.astype(f32)).reshape(b, l, M2_GROUPS, M2_HPG)
    a_head = (-jnp.exp(m2_a_log.astype(f32))).reshape(M2_GROUPS, M2_HPG)
    y = ssd_chunked(xs, dt, dt * a_head, bm, cm)
    y = y + m2_d.astype(f32).reshape(M2_GROUPS, M2_HPG)[:, :, None] * xs
    y = y.reshape(b, l, M2_W) * jax.nn.silu(m2_z.astype(f32))
    y = y.reshape(b, l, M2_GROUPS, M2_W // M2_GROUPS)
    y = (y * lax.rsqrt(jnp.mean(y * y, -1, keepdims=True) + NORM_EPS)).reshape(b, l, M2_W)
    y = y * m2_norm_w.astype(f32)

    mixed = jnp.concatenate([o, y], axis=-1).astype(h.dtype)
    return mixed @ w_out


def conv_ffn(h, w_up, conv_w, w_down):
    u = causal_dwconv(h @ w_up, conv_w)
    gate, val = jnp.split(u, 2, axis=-1)
    return (jax.nn.silu(gate) * val) @ w_down


def setup_inputs(seed: int = 0) -> dict:
    key = jax.random.key(seed)
    ks = jax.random.split(key, 24)
    nrm = jax.random.normal

    def dt_bias(k, n):
        dt = jnp.exp(jax.random.uniform(k, (DEPTH, n), minval=math.log(1e-3), maxval=math.log(1e-1)))
        return dt + jnp.log(-jnp.expm1(-dt))

    def a_log(k, n):
        return jnp.log(jax.random.uniform(k, (DEPTH, n), minval=1.0, maxval=16.0))

    return {
        "x": nrm(ks[0], (BATCH, SEQ, D_MODEL), jnp.float32),
        "meta_tokens": nrm(ks[1], (N_META, D_MODEL), jnp.float32),
        "norm_mix_w": 1.0 + 0.02 * nrm(ks[2], (DEPTH, D_MODEL), jnp.float32),
        "w_in": nrm(ks[3], (DEPTH, D_MODEL, D_IN_PROJ), jnp.float32) * D_MODEL ** -0.5,
        "dn_conv_w": nrm(ks[4], (DEPTH, DN_CONV, 3 * DN_W), jnp.float32) * DN_CONV ** -0.5,
        "dn_a_log": a_log(ks[5], DN_HEADS),
        "dn_dt_bias": dt_bias(ks[6], DN_HEADS),
        "dn_norm_w": 1.0 + 0.02 * nrm(ks[7], (DEPTH, DN_DV), jnp.float32),
        "m2_conv_w": nrm(ks[8], (DEPTH, M2_CONV, M2_W + 2 * M2_GROUPS * M2_STATE), jnp.float32) * M2_CONV ** -0.5,
        "m2_conv_b": 0.02 * nrm(ks[9], (DEPTH, M2_W + 2 * M2_GROUPS * M2_STATE), jnp.float32),
        "m2_a_log": a_log(ks[10], M2_HEADS),
        "m2_dt_bias": dt_bias(ks[11], M2_HEADS),
        "m2_d": 1.0 + 0.1 * nrm(ks[12], (DEPTH, M2_HEADS), jnp.float32),
        "m2_norm_w": 1.0 + 0.02 * nrm(ks[13], (DEPTH, M2_W), jnp.float32),
        "w_out": nrm(ks[14], (DEPTH, D_MIX, D_MODEL), jnp.float32) * D_MIX ** -0.5,
        "norm_ffn_w": 1.0 + 0.02 * nrm(ks[15], (DEPTH, D_MODEL), jnp.float32),
        "ffn_up": nrm(ks[16], (DEPTH, D_MODEL, 2 * D_FF), jnp.float32) * D_MODEL ** -0.5,
        "ffn_conv_w": nrm(ks[17], (DEPTH, FFN_CONV, 2 * D_FF), jnp.float32) * FFN_CONV ** -0.5,
        "ffn_down": nrm(ks[18], (DEPTH, D_FF, D_MODEL), jnp.float32) * D_FF ** -0.5,
        "norm_final_w": 1.0 + 0.02 * nrm(ks[19], (D_MODEL,), jnp.float32),
    }


def reference(x, meta_tokens, norm_mix_w, w_in, dn_conv_w, dn_a_log, dn_dt_bias, dn_norm_w,
              m2_conv_w, m2_conv_b, m2_a_log, m2_dt_bias, m2_d, m2_norm_w, w_out,
              norm_ffn_w, ffn_up, ffn_conv_w, ffn_down, norm_final_w):
    b = x.shape[0]
    meta = jnp.broadcast_to(meta_tokens[None].astype(x.dtype), (b, N_META, D_MODEL))
    h = jnp.concatenate([meta, x], axis=1)
    for layer in range(DEPTH):
        h = h + hybrid_mixer(rms_norm(h, norm_mix_w[layer]), w_in[layer], dn_conv_w[layer],
                             dn_a_log[layer], dn_dt_bias[layer], dn_norm_w[layer],
                             m2_conv_w[layer], m2_conv_b[layer], m2_a_log[layer],
                             m2_dt_bias[layer], m2_d[layer], m2_norm_w[layer], w_out[layer])
        h = h + conv_ffn(rms_norm(h, norm_ffn_w[layer]), ffn_up[layer], ffn_conv_w[layer],
                         ffn_down[layer])
    return rms_norm(h, norm_final_w)[:, N_META:]
```

```python
import functools

import jax
import jax.numpy as jnp
from jax import lax
from jax.experimental import pallas as pl
from jax.experimental.pallas import tpu as pltpu

F32 = jnp.float32
BF16 = jnp.bfloat16

NORM_EPS = 1e-6
N_META = 16
CHUNK = 64
LANES = 128
TAIL = 8

DN_HEADS = 16
DN_DK = 128
M2_HEADS = 32
M2_P = 64
M2_GROUPS = 4
M2_STATE = 128
M2_GW = M2_HEADS * M2_P // M2_GROUPS

D_MODEL = 2048
COL_Q, COL_K, COL_V = 0, 2048, 4096
COL_DNZ = 6144
COL_M2Z = 8192
COL_XS = 10240
COL_B = 12288
COL_C = 12800
N_BIG = 13312
SM_BETA, SM_A, SM_DT = 0, 16, 32

VMEM_LIMIT = 56 * 1024 * 1024


def _cparams(n_axes):
    return pltpu.CompilerParams(dimension_semantics=("arbitrary",) * n_axes,
                                vmem_limit_bytes=VMEM_LIMIT)


def _dot(a, b):
    return jnp.dot(a, b, preferred_element_type=F32)


def _dot_nt(a, b):
    return lax.dot_general(a, b, (((1,), (1,)), ((), ())), preferred_element_type=F32)


def _sigmoid(x):
    return 1.0 / (1.0 + jnp.exp(-x))


def _silu(x):
    return x * _sigmoid(x)


def _softplus(x):
    return jnp.maximum(x, 0.0) + jnp.log1p(jnp.exp(-jnp.abs(x)))


def _cumsum_rows(x):
    n = x.shape[0]
    row = lax.broadcasted_iota(jnp.int32, x.shape, 0)
    s = 1
    while s < n:
        x = x + jnp.where(row >= s, pltpu.roll(x, s, axis=0), 0.0)
        s *= 2
    return x


def _causal_conv(x, tail, w, k):
    xe = jnp.concatenate([tail, x], axis=0)
    acc = x * w[k - 1:k]
    for j in range(k - 1):
        acc = acc + pltpu.roll(xe, k - 1 - j, axis=0)[TAIL:] * w[j:j + 1]
    return acc


def _norm_matmul_kernel(x_ref, g_ref, w_ref, ws_ref, o_ref, os_ref, xn_ref):
    @pl.when(pl.program_id(1) == 0)
    def _():
        x = x_ref[...]
        ms = jnp.mean(x * x, axis=-1, keepdims=True)
        xn = (x * lax.rsqrt(ms + NORM_EPS) * g_ref[...]).astype(BF16)
        xn_ref[...] = xn
        os_ref[...] = _dot(xn, ws_ref[...])

    o_ref[...] = _dot(xn_ref[...], w_ref[...]).astype(o_ref.dtype)


def norm_matmul(x, gain, w, w_small, *, tm, tn, out_dtype):
    rows, k = x.shape
    n = w.shape[1]
    ns = w_small.shape[1]
    return pl.pallas_call(
        _norm_matmul_kernel,
        grid=(rows // tm, n // tn),
        in_specs=[
            pl.BlockSpec((tm, k), lambda i, j: (i, 0)),
            pl.BlockSpec((1, k), lambda i, j: (0, 0)),
            pl.BlockSpec((k, tn), lambda i, j: (0, j)),
            pl.BlockSpec((k, ns), lambda i, j: (0, 0)),
        ],
        out_specs=[
            pl.BlockSpec((tm, tn), lambda i, j: (i, j)),
            pl.BlockSpec((tm, ns), lambda i, j: (i, 0)),
        ],
        out_shape=[
            jax.ShapeDtypeStruct((rows, n), out_dtype),
            jax.ShapeDtypeStruct((rows, ns), F32),
        ],
        scratch_shapes=[pltpu.VMEM((tm, k), BF16)],
        compiler_params=_cparams(2),
        name="norm_matmul",
    )(x, gain, w, w_small)


def _gdn_kernel(q_ref, k_ref, v_ref, z_ref, sm_ref, wq_ref, wk_ref, wv_ref, alog_ref, dtb_ref,
                nw_ref, s0_ref, tq_ref, tk_ref, tv_ref,
                o_ref, sfin_ref, s_sc, tail_sc, *, cps):
    c = pl.program_id(1)
    rb = cps * CHUNK

    @pl.when(c == 0)
    def _():
        s_sc[...] = s0_ref[...]
        tail_sc[0] = tq_ref[...]
        tail_sc[1] = tk_ref[...]
        tail_sc[2] = tv_ref[...]

    def conv_silu(x_ref, w_ref, idx):
        x = x_ref[...].astype(F32)
        y = _causal_conv(x, tail_sc[idx], w_ref[...], 4)
        tail_sc[idx] = x[rb - TAIL:rb]
        return _silu(y)

    qs = conv_silu(q_ref, wq_ref, 0)
    ks = conv_silu(k_ref, wk_ref, 1)
    vs = conv_silu(v_ref, wv_ref, 2)

    sm = sm_ref[...]
    beta_all = _sigmoid(sm)
    g_all = -jnp.exp(alog_ref[...]) * _softplus(sm + dtb_ref[...])
    nw = nw_ref[...]

    ri = lax.broadcasted_iota(jnp.int32, (CHUNK, CHUNK), 0)
    ci = lax.broadcasted_iota(jnp.int32, (CHUNK, CHUNK), 1)
    causal = ri >= ci
    strict = ri > ci
    eye = (ri == ci).astype(F32)

    for cc in range(cps):
        r = slice(cc * CHUNK, (cc + 1) * CHUNK)
        gc = _cumsum_rows(g_all[r])
        gct = jnp.concatenate([gc, gc], axis=0).T
        beta_c = beta_all[r]
        for h in range(DN_HEADS):
            sl = slice(h * DN_DK, (h + 1) * DN_DK)
            q = qs[r, sl]
            k = ks[r, sl]
            v = vs[r, sl]
            q = q * (lax.rsqrt(jnp.sum(q * q, axis=-1, keepdims=True) + NORM_EPS) * DN_DK ** -0.5)
            k = k * lax.rsqrt(jnp.sum(k * k, axis=-1, keepdims=True) + NORM_EPS)
            beta = beta_c[:, SM_BETA + h:SM_BETA + h + 1]
            gcol = gc[:, SM_A + h:SM_A + h + 1]
            grow = gct[SM_A + h:SM_A + h + 1, :CHUNK]
            glast = gcol[CHUNK - 1:CHUNK]
            decay = jnp.exp(jnp.where(causal, gcol - grow, -jnp.inf))
            q16 = q.astype(BF16)
            k16 = k.astype(BF16)
            gram = _dot_nt(jnp.concatenate([q16, k16], axis=0), k16)
            qkm = gram[:CHUNK] * decay
            a_mat = jnp.where(strict, beta * gram[CHUNK:] * decay, 0.0)
            x_inv = eye - a_mat
            a16 = a_mat.astype(BF16)
            p = _dot(a16, a16)
            for lvl in range(5):
                p16 = p.astype(BF16)
                if lvl < 4:
                    xp = _dot(jnp.concatenate([x_inv.astype(BF16), p16], axis=0), p16)
                    x_inv = x_inv + xp[:CHUNK]
                    p = xp[CHUNK:]
                else:
                    x_inv = x_inv + _dot(x_inv.astype(BF16), p16)
            eg = jnp.exp(gcol)
            rhs = jnp.concatenate([v * beta, k * (beta * eg)], axis=1).astype(BF16)
            wu = _dot(x_inv.astype(BF16), rhs)
            u = wu[:, :DN_DK]
            w = wu[:, DN_DK:]
            s = s_sc[h]
            s16 = s.astype(BF16)
            v_new = u - _dot(w.astype(BF16), s16)
            vn16 = v_new.astype(BF16)
            o = _dot((q * eg).astype(BF16), s16) + _dot(qkm.astype(BF16), vn16)
            kd = k * jnp.exp(glast - gcol)
            s_sc[h] = s * jnp.exp(glast) + _dot(kd.T.astype(BF16), vn16)
            z = z_ref[r, sl].astype(F32)
            on = o * lax.rsqrt(jnp.mean(o * o, axis=-1, keepdims=True) + NORM_EPS) * nw * _silu(z)
            o_ref[r, sl] = on.astype(o_ref.dtype)

    @pl.when(c == pl.num_programs(1) - 1)
    def _():
        sfin_ref[0] = s_sc[...]


def gdn_scan(proj, small, conv_w, alog_row, dtb_row, norm_w, s0, tail, *, batch, cps):
    rows = proj.shape[0]
    rb = cps * CHUNK
    ncb = rows // batch // rb
    w = DN_HEADS * DN_DK

    def col(off):
        return lambda b, c: (b * ncb + c, off // w)

    kern = functools.partial(_gdn_kernel, cps=cps)
    return pl.pallas_call(
        kern,
        grid=(batch, ncb),
        in_specs=[
            pl.BlockSpec((rb, w), col(COL_Q)),
            pl.BlockSpec((rb, w), col(COL_K)),
            pl.BlockSpec((rb, w), col(COL_V)),
            pl.BlockSpec((rb, w), col(COL_DNZ)),
            pl.BlockSpec((rb, LANES), lambda b, c: (b * ncb + c, 0)),
            pl.BlockSpec((4, w), lambda b, c: (0, 0)),
            pl.BlockSpec((4, w), lambda b, c: (0, 1)),
            pl.BlockSpec((4, w), lambda b, c: (0, 2)),
            pl.BlockSpec((1, LANES), lambda b, c: (0, 0)),
            pl.BlockSpec((1, LANES), lambda b, c: (0, 0)),
            pl.BlockSpec((1, DN_DK), lambda b, c: (0, 0)),
            pl.BlockSpec((DN_HEADS, DN_DK, DN_DK), lambda b, c: (0, 0, 0)),
            pl.BlockSpec((TAIL, w), lambda b, c: (0, 0)),
            pl.BlockSpec((TAIL, w), lambda b, c: (0, 1)),
            pl.BlockSpec((TAIL, w), lambda b, c: (0, 2)),
        ],
        out_specs=[
            pl.BlockSpec((rb, w), lambda b, c: (b * ncb + c, 0)),
            pl.BlockSpec((1, DN_HEADS, DN_DK, DN_DK), lambda b, c: (b, 0, 0, 0)),
        ],
        out_shape=[
            jax.ShapeDtypeStruct((rows, w), BF16),
            jax.ShapeDtypeStruct((batch, DN_HEADS, DN_DK, DN_DK), F32),
        ],
        scratch_shapes=[
            pltpu.VMEM((DN_HEADS, DN_DK, DN_DK), F32),
            pltpu.VMEM((3, TAIL, w), F32),
        ],
        compiler_params=_cparams(2),
        name="gdn_scan",
    )(proj, proj, proj, proj, small, conv_w, conv_w, conv_w, alog_row, dtb_row, norm_w, s0,
      tail, tail, tail)


def _ssd_kernel(x_ref, b_ref, c_ref, z_ref, sm_ref, wx_ref, wb_ref, wc_ref, bx_ref, bb_ref, bc_ref,
                alog_ref, dtb_ref, d_ref, nw_ref, st0_ref, tx_ref, tb_ref, tc_ref,
                y_ref, stfin_ref, st_sc, tx_sc, tb_sc, tc_sc, *, cps, n_pad):
    c = pl.program_id(1)
    rb = cps * CHUNK

    @pl.when(c == 0)
    def _():
        st_sc[...] = st0_ref[...]
        tx_sc[...] = tx_ref[...]
        tb_sc[...] = tb_ref[...]
        tc_sc[...] = tc_ref[...]

    def conv_silu(x_ref, w_ref, bias_ref, t_sc):
        x = x_ref[...].astype(F32)
        y = _causal_conv(x, t_sc[...], w_ref[...], 4) + bias_ref[...]
        t_sc[...] = x[rb - TAIL:rb]
        return _silu(y)

    xs = conv_silu(x_ref, wx_ref, bx_ref, tx_sc)
    bm = conv_silu(b_ref, wb_ref, bb_ref, tb_sc)
    cm = conv_silu(c_ref, wc_ref, bc_ref, tc_sc)

    sm = sm_ref[...]
    dt_all = _softplus(sm + dtb_ref[...])
    a_all = dt_all * (-jnp.exp(alog_ref[...]))
    d_row = d_ref[...]
    nw = nw_ref[...]

    lane = lax.broadcasted_iota(jnp.int32, (CHUNK, LANES), 1)
    rowi = lax.broadcasted_iota(jnp.int32, (CHUNK, LANES), 0)
    lo = lane < M2_P
    causal2 = rowi >= jnp.where(lo, lane, lane - M2_P)
    lo2 = lax.broadcasted_iota(jnp.int32, (2 * CHUNK, LANES), 1) < M2_P
    top2 = lax.broadcasted_iota(jnp.int32, (2 * CHUNK, LANES), 0) < CHUNK
    blk2 = lo2 == top2

    npair = M2_HEADS // 2
    ppg = npair // M2_GROUPS

    for cc in range(cps):
        r = slice(cc * CHUNK, (cc + 1) * CHUNK)
        acs = _cumsum_rows(a_all[r])
        zt = jnp.concatenate([acs, pltpu.roll(acs, LANES - 1, axis=1)], axis=0).T
        dt = dt_all[r]
        y_tiles = []
        for g in range(M2_GROUPS):
            gs = slice(g * M2_STATE, (g + 1) * M2_STATE)
            b16 = bm[r, gs].astype(BF16)
            c16 = cm[r, gs].astype(BF16)
            cbcb = _dot_nt(c16, jnp.concatenate([b16, b16], axis=0))
            st = st_sc[g]
            yoff = _dot(c16, st.astype(BF16))
            xdec_tiles = []
            ea_last = []
            for mm in range(ppg):
                m = g * ppg + mm
                la = SM_DT + 2 * m
                ea = jnp.where(lo, acs[:, la:la + 1], acs[:, la + 1:la + 2])
                edt = jnp.where(lo, dt[:, la:la + 1], dt[:, la + 1:la + 2])
                xs_m = xs[r, m * LANES:(m + 1) * LANES]
                xdt = xs_m * edt
                if n_pad:
                    xdt = jnp.where(rowi + cc * CHUNK >= n_pad, xdt, 0.0)
                rowp = zt[la:la + 1, :]
                lmat = jnp.exp(jnp.where(causal2, ea - rowp, -jnp.inf))
                wp = (cbcb * lmat).astype(BF16)
                x16 = xdt.astype(BF16)
                x2 = jnp.concatenate([x16, x16], axis=0)
                rhs = jnp.where(blk2, x2, jnp.zeros_like(x2))
                y = _dot(wp, rhs) + yoff[:, mm * LANES:(mm + 1) * LANES] * jnp.exp(ea)
                y_tiles.append(y + d_row[:, m * LANES:(m + 1) * LANES] * xs_m)
                eal = ea[CHUNK - 1:CHUNK]
                ea_last.append(eal)
                xdec_tiles.append((xdt * jnp.exp(eal - ea)).astype(BF16))
            xdec = jnp.concatenate(xdec_tiles, axis=1)
            sdec = jnp.exp(jnp.concatenate(ea_last, axis=1))
            bt16 = bm[r, gs].T.astype(BF16)
            st_sc[g] = st * sdec + _dot(bt16, xdec)
        for g in range(M2_GROUPS):
            tiles = []
            ssq = None
            for mm in range(ppg):
                m = g * ppg + mm
                z = z_ref[r, m * LANES:(m + 1) * LANES].astype(F32)
                t = y_tiles[m] * _silu(z)
                tiles.append(t)
                part = jnp.sum(t * t, axis=-1, keepdims=True)
                ssq = part if ssq is None else ssq + part
            scale = lax.rsqrt(ssq * (1.0 / M2_GW) + NORM_EPS)
            for mm in range(ppg):
                m = g * ppg + mm
                cs = slice(m * LANES, (m + 1) * LANES)
                y_ref[r, cs] = (tiles[mm] * scale * nw[:, cs]).astype(y_ref.dtype)

    @pl.when(c == pl.num_programs(1) - 1)
    def _():
        stfin_ref[0] = st_sc[...]


def ssd_scan(proj, small, conv_w, conv_b, alog_row, dtb_row, d_row, norm_w, st0, tail, *,
             batch, cps, n_pad):
    rows = proj.shape[0]
    rb = cps * CHUNK
    ncb = rows // batch // rb
    wx = M2_HEADS * M2_P
    wg = M2_GROUPS * M2_STATE

    def rowblk(width, off):
        return pl.BlockSpec((rb, width), lambda b, c: (b * ncb + c, off // width))

    def const(shape, *idx):
        return pl.BlockSpec(shape, lambda b, c: idx)

    kern = functools.partial(_ssd_kernel, cps=cps, n_pad=n_pad)
    return pl.pallas_call(
        kern,
        grid=(batch, ncb),
        in_specs=[
            rowblk(wx, COL_XS),
            rowblk(wg, COL_B),
            rowblk(wg, COL_C),
            rowblk(wx, COL_M2Z),
            pl.BlockSpec((rb, LANES), lambda b, c: (b * ncb + c, 0)),
            const((4, wx), 0, 0),
            const((4, wg), 0, wx // wg),
            const((4, wg), 0, wx // wg + 1),
            const((1, wx), 0, 0),
            const((1, wg), 0, wx // wg),
            const((1, wg), 0, wx // wg + 1),
            const((1, LANES), 0, 0),
            const((1, LANES), 0, 0),
            const((1, wx), 0, 0),
            const((1, wx), 0, 0),
            const((M2_GROUPS, M2_STATE, M2_GW), 0, 0, 0),
            const((TAIL, wx), 0, 0),
            const((TAIL, wg), 0, wx // wg),
            const((TAIL, wg), 0, wx // wg + 1),
        ],
        out_specs=[
            pl.BlockSpec((rb, wx), lambda b, c: (b * ncb + c, 0)),
            pl.BlockSpec((1, M2_GROUPS, M2_STATE, M2_GW), lambda b, c: (b, 0, 0, 0)),
        ],
        out_shape=[
            jax.ShapeDtypeStruct((rows, wx), BF16),
            jax.ShapeDtypeStruct((batch, M2_GROUPS, M2_STATE, M2_GW), F32),
        ],
        scratch_shapes=[
            pltpu.VMEM((M2_GROUPS, M2_STATE, M2_GW), F32),
            pltpu.VMEM((TAIL, wx), F32),
            pltpu.VMEM((TAIL, wg), F32),
            pltpu.VMEM((TAIL, wg), F32),
        ],
        compiler_params=_cparams(2),
        name="ssd_scan",
    )(proj, proj, proj, proj, small, conv_w, conv_w, conv_w, conv_b, conv_b, conv_b,
      alog_row, dtb_row, d_row, norm_w, st0, tail, tail, tail)


def _out_proj_kernel(x_ref, o_ref, y_ref, wo_ref, wy_ref, h_ref):
    h_ref[...] = x_ref[...] + _dot(o_ref[...], wo_ref[...]) + _dot(y_ref[...], wy_ref[...])


def out_proj(x, o, y, w_out, *, tm, tn):
    rows, d = x.shape
    km = o.shape[1]
    return pl.pallas_call(
        _out_proj_kernel,
        grid=(d // tn, rows // tm),
        in_specs=[
            pl.BlockSpec((tm, tn), lambda j, i: (i, j)),
            pl.BlockSpec((tm, km), lambda j, i: (i, 0)),
            pl.BlockSpec((tm, km), lambda j, i: (i, 0)),
            pl.BlockSpec((km, tn), lambda j, i: (0, j)),
            pl.BlockSpec((km, tn), lambda j, i: (1, j)),
        ],
        out_specs=pl.BlockSpec((tm, tn), lambda j, i: (i, j)),
        out_shape=jax.ShapeDtypeStruct((rows, d), F32),
        compiler_params=_cparams(2),
        name="out_proj",
    )(x, o, y, w_out, w_out)


def _conv_ffn_kernel(h_ref, gn_ref, wg_ref, wv_ref, cwg_ref, cwv_ref, wd_ref, tg_ref, tv_ref,
                     fn_ref, o_ref, hn_sc, acc_sc, cg_sc, cv_sc, *, tiles_per_batch):
    i = pl.program_id(0)
    f = pl.program_id(1)
    nf = pl.num_programs(1)
    tm = h_ref.shape[0]

    @pl.when(f == 0)
    def _():
        h = h_ref[...]
        ms = jnp.mean(h * h, axis=-1, keepdims=True)
        hn_sc[...] = (h * lax.rsqrt(ms + NORM_EPS) * gn_ref[...]).astype(BF16)

    first = (i % tiles_per_batch) == 0
    hn = hn_sc[...]

    def branch(w_ref, cw_ref, t_ref, c_sc):
        u = _dot(hn, w_ref[...])
        tail = jnp.where(first, t_ref[...], c_sc[f])
        y = _causal_conv(u, tail, cw_ref[...], 3)
        c_sc[f] = u[tm - TAIL:tm]
        return y

    gate = branch(wg_ref, cwg_ref, tg_ref, cg_sc)
    val = branch(wv_ref, cwv_ref, tv_ref, cv_sc)
    act = (_silu(gate) * val).astype(BF16)
    contrib = _dot(act, wd_ref[...])

    @pl.when(f == 0)
    def _():
        acc_sc[...] = contrib

    @pl.when(f > 0)
    def _():
        acc_sc[...] += contrib

    @pl.when(f == nf - 1)
    def _():
        h2 = h_ref[...] + acc_sc[...]
        ms = jnp.mean(h2 * h2, axis=-1, keepdims=True)
        o_ref[...] = h2 * lax.rsqrt(ms + NORM_EPS) * fn_ref[...]


def conv_ffn(h, gain, w_up, conv_w, w_down, u_tail, final_gain, *, tm, tf, tiles_per_batch):
    rows, d = h.shape
    dff = w_down.shape[0]
    nf = dff // tf
    kern = functools.partial(_conv_ffn_kernel, tiles_per_batch=tiles_per_batch)
    return pl.pallas_call(
        kern,
        grid=(rows // tm, nf),
        in_specs=[
            pl.BlockSpec((tm, d), lambda i, f: (i, 0)),
            pl.BlockSpec((1, d), lambda i, f: (0, 0)),
            pl.BlockSpec((d, tf), lambda i, f: (0, f)),
            pl.BlockSpec((d, tf), lambda i, f: (0, nf + f)),
            pl.BlockSpec((3, tf), lambda i, f: (0, f)),
            pl.BlockSpec((3, tf), lambda i, f: (0, nf + f)),
            pl.BlockSpec((tf, d), lambda i, f: (f, 0)),
            pl.BlockSpec((TAIL, tf), lambda i, f: (0, f)),
            pl.BlockSpec((TAIL, tf), lambda i, f: (0, nf + f)),
            pl.BlockSpec((1, d), lambda i, f: (0, 0)),
        ],
        out_specs=pl.BlockSpec((tm, d), lambda i, f: (i, 0)),
        out_shape=jax.ShapeDtypeStruct((rows, d), F32),
        scratch_shapes=[
            pltpu.VMEM((tm, d), BF16),
            pltpu.VMEM((tm, d), F32),
            pltpu.VMEM((nf, TAIL, tf), F32),
            pltpu.VMEM((nf, TAIL, tf), F32),
        ],
        compiler_params=_cparams(2),
        name="conv_ffn",
    )(h, gain, w_up, w_up, conv_w, conv_w, w_down, u_tail, u_tail, final_gain)


def _row_tile(rows, pref):
    t = min(rows, pref)
    while rows % t:
        t //= 2
    return t


def _lane_row(vec, offset):
    return jnp.zeros((1, LANES), F32).at[0, offset:offset + vec.shape[0]].set(vec.astype(F32))


def kernel(x, meta_tokens, norm_mix_w, w_in, dn_conv_w, dn_a_log, dn_dt_bias, dn_norm_w, m2_conv_w, m2_conv_b, m2_a_log, m2_dt_bias, m2_d, m2_norm_w, w_out, norm_ffn_w, ffn_up, ffn_conv_w, ffn_down, norm_final_w):
    batch, seq, d = x.shape
    depth = w_in.shape[0]
    assert depth == 1 and d == D_MODEL and seq % CHUNK == 0
    rows = batch * seq
    pad = CHUNK - N_META

    w = w_in[0]
    o_qkv, o_z, o_b, o_a = 0, 6144, 8192, 8208
    o_m2z, o_xbc, o_dt = 8224, 10272, 13344
    w_big = jnp.concatenate(
        [w[:, o_qkv:o_z], w[:, o_z:o_b], w[:, o_m2z:o_xbc], w[:, o_xbc:o_dt]], axis=1).astype(BF16)
    w_small = jnp.concatenate(
        [w[:, o_b:o_a], w[:, o_a:o_m2z], w[:, o_dt:], jnp.zeros((d, LANES - 64), w.dtype)],
        axis=1).astype(BF16)
    w_out16 = w_out[0].astype(BF16)
    ffn_up16 = ffn_up[0].astype(BF16)
    ffn_down16 = ffn_down[0].astype(BF16)
    dff = ffn_down16.shape[0]
    zero_small = jnp.zeros((d, LANES), BF16)

    gain_mix = norm_mix_w[0].reshape(1, d).astype(F32)
    gain_ffn = norm_ffn_w[0].reshape(1, d).astype(F32)
    gain_fin = norm_final_w.reshape(1, d).astype(F32)
    dn_alog_row = _lane_row(dn_a_log[0], SM_A)
    dn_dtb_row = _lane_row(dn_dt_bias[0], SM_A)
    m2_alog_row = _lane_row(m2_a_log[0], SM_DT)
    m2_dtb_row = _lane_row(m2_dt_bias[0], SM_DT)
    dn_cw = dn_conv_w[0].astype(F32)
    dn_nw = dn_norm_w[0].reshape(1, DN_DK).astype(F32)
    m2_cw = m2_conv_w[0].astype(F32)
    m2_cb = m2_conv_b[0].reshape(1, -1).astype(F32)
    m2_d_row = jnp.repeat(m2_d[0].astype(F32), M2_P).reshape(1, M2_HEADS * M2_P)
    m2_nw = m2_norm_w[0].reshape(1, -1).astype(F32)
    ffn_cw = ffn_conv_w[0].astype(F32)

    hm = jnp.concatenate([jnp.zeros((pad, d), F32), meta_tokens.astype(F32)], axis=0)
    pm_big, pm_small = norm_matmul(hm, gain_mix, w_big, w_small, tm=CHUNK, tn=1024, out_dtype=BF16)
    zeros_tail = jnp.zeros((TAIL, N_BIG), F32)
    o_m, s_meta = gdn_scan(pm_big, pm_small, dn_cw, dn_alog_row, dn_dtb_row, dn_nw,
                           jnp.zeros((DN_HEADS, DN_DK, DN_DK), F32), zeros_tail[:, :3 * 2048],
                           batch=1, cps=1)
    y_m, st_meta = ssd_scan(pm_big, pm_small, m2_cw, m2_cb, m2_alog_row, m2_dtb_row, m2_d_row, m2_nw,
                            jnp.zeros((M2_GROUPS, M2_STATE, M2_GW), F32),
                            zeros_tail[:, :3072], batch=1, cps=1, n_pad=pad)
    h1_m = out_proj(hm, o_m, y_m, w_out16, tm=CHUNK, tn=1024)
    u_m, _ = norm_matmul(h1_m, gain_ffn, ffn_up16, zero_small, tm=CHUNK, tn=1024, out_dtype=F32)
    tail_big = pm_big[CHUNK - TAIL:].astype(F32)
    tail_qkv = tail_big[:, COL_Q:COL_DNZ]
    tail_xbc = tail_big[:, COL_XS:N_BIG]
    tail_u = u_m[CHUNK - TAIL:]

    x2 = x.reshape(rows, d).astype(F32)
    p_big, p_small = norm_matmul(x2, gain_mix, w_big, w_small, tm=_row_tile(rows, 1024), tn=1024,
                                 out_dtype=BF16)
    o_g, _ = gdn_scan(p_big, p_small, dn_cw, dn_alog_row, dn_dtb_row, dn_nw, s_meta[0], tail_qkv,
                      batch=batch, cps=1)
    y_g, _ = ssd_scan(p_big, p_small, m2_cw, m2_cb, m2_alog_row, m2_dtb_row, m2_d_row, m2_nw,
                      st_meta[0], tail_xbc, batch=batch, cps=1, n_pad=0)
    h1 = out_proj(x2, o_g, y_g, w_out16, tm=_row_tile(rows, 512), tn=1024)
    tm_ffn = _row_tile(seq, 512)
    out = conv_ffn(h1, gain_ffn, ffn_up16, ffn_cw, ffn_down16, tail_u, gain_fin,
                   tm=tm_ffn, tf=512, tiles_per_batch=seq // tm_ffn)
    return out.reshape(batch, seq, d)
```

```python
import functools

import jax
import jax.numpy as jnp
from jax import lax
from jax.experimental import pallas as pl
from jax.experimental.pallas import tpu as pltpu

F32 = jnp.float32
BF16 = jnp.bfloat16

NORM_EPS = 1e-6
N_META = 16
CHUNK = 64
LANES = 128
TAIL = 8

DN_HEADS = 16
DN_DK = 128
M2_HEADS = 32
M2_P = 64
M2_GROUPS = 4
M2_STATE = 128
M2_GW = M2_HEADS * M2_P // M2_GROUPS

D_MODEL = 2048
COL_Q, COL_K, COL_V = 0, 2048, 4096
COL_DNZ = 6144
COL_M2Z = 8192
COL_XS = 10240
COL_B = 12288
COL_C = 12800
N_BIG = 13312
SM_BETA, SM_A, SM_DT = 0, 16, 32

VMEM_LIMIT = 56 * 1024 * 1024


def _cparams(n_axes):
    return pltpu.CompilerParams(dimension_semantics=("arbitrary",) * n_axes,
                                vmem_limit_bytes=VMEM_LIMIT)


def _dot(a, b):
    return jnp.dot(a, b, preferred_element_type=F32)


def _dot_nt(a, b):
    return lax.dot_general(a, b, (((1,), (1,)), ((), ())), preferred_element_type=F32)


def _bdot(a, b):
    return lax.dot_general(a, b, (((2,), (1,)), ((0,), (0,))), preferred_element_type=F32)


def _bdot_nt(a, b):
    return lax.dot_general(a, b, (((2,), (2,)), ((0,), (0,))), preferred_element_type=F32)


def _sigmoid(x):
    return 1.0 / (1.0 + jnp.exp(-x))


def _silu(x):
    return x * _sigmoid(x)


def _softplus(x):
    return jnp.maximum(x, 0.0) + jnp.log1p(jnp.exp(-jnp.abs(x)))


def _cumsum_rows(x):
    n = x.shape[0]
    row = lax.broadcasted_iota(jnp.int32, x.shape, 0)
    s = 1
    while s < n:
        x = x + jnp.where(row >= s, pltpu.roll(x, s, axis=0), 0.0)
        s *= 2
    return x


def _causal_conv(x, tail, w, k):
    xe = jnp.concatenate([tail, x], axis=0)
    acc = x * w[k - 1:k]
    for j in range(k - 1):
        acc = acc + pltpu.roll(xe, k - 1 - j, axis=0)[TAIL:] * w[j:j + 1]
    return acc


def _norm_matmul_kernel(x_ref, g_ref, w_ref, ws_ref, o_ref, os_ref, xn_ref):
    @pl.when(pl.program_id(1) == 0)
    def _():
        x = x_ref[...]
        ms = jnp.mean(x * x, axis=-1, keepdims=True)
        xn = (x * lax.rsqrt(ms + NORM_EPS) * g_ref[...]).astype(BF16)
        xn_ref[...] = xn
        os_ref[...] = _dot(xn, ws_ref[...])

    o_ref[...] = _dot(xn_ref[...], w_ref[...]).astype(o_ref.dtype)


def norm_matmul(x, gain, w, w_small, *, tm, tn, out_dtype):
    rows, k = x.shape
    n = w.shape[1]
    ns = w_small.shape[1]
    return pl.pallas_call(
        _norm_matmul_kernel,
        grid=(rows // tm, n // tn),
        in_specs=[
            pl.BlockSpec((tm, k), lambda i, j: (i, 0)),
            pl.BlockSpec((1, k), lambda i, j: (0, 0)),
            pl.BlockSpec((k, tn), lambda i, j: (0, j)),
            pl.BlockSpec((k, ns), lambda i, j: (0, 0)),
        ],
        out_specs=[
            pl.BlockSpec((tm, tn), lambda i, j: (i, j)),
            pl.BlockSpec((tm, ns), lambda i, j: (i, 0)),
        ],
        out_shape=[
            jax.ShapeDtypeStruct((rows, n), out_dtype),
            jax.ShapeDtypeStruct((rows, ns), F32),
        ],
        scratch_shapes=[pltpu.VMEM((tm, k), BF16)],
        compiler_params=_cparams(2),
        name="norm_matmul",
    )(x, gain, w, w_small)


def _gdn_kernel(q_ref, k_ref, v_ref, z_ref, sm_ref, wq_ref, wk_ref, wv_ref, alog_ref, dtb_ref,
                nw_ref, s0_ref, tq_ref, tk_ref, tv_ref,
                o_ref, sfin_ref, s_sc, tail_sc, *, cps):
    c = pl.program_id(1)
    rb = cps * CHUNK

    @pl.when(c == 0)
    def _():
        s_sc[...] = s0_ref[...]
        tail_sc[0] = tq_ref[...]
        tail_sc[1] = tk_ref[...]
        tail_sc[2] = tv_ref[...]

    def conv_silu(x_ref, w_ref, idx):
        x = x_ref[...].astype(F32)
        y = _causal_conv(x, tail_sc[idx], w_ref[...], 4)
        tail_sc[idx] = x[rb - TAIL:rb]
        return _silu(y)

    qs = conv_silu(q_ref, wq_ref, 0)
    ks = conv_silu(k_ref, wk_ref, 1)
    vs = conv_silu(v_ref, wv_ref, 2)

    sm = sm_ref[...]
    beta_all = _sigmoid(sm)
    g_all = -jnp.exp(alog_ref[...]) * _softplus(sm + dtb_ref[...])
    nw = nw_ref[...]

    ri = lax.broadcasted_iota(jnp.int32, (CHUNK, CHUNK), 0)
    ci = lax.broadcasted_iota(jnp.int32, (CHUNK, CHUNK), 1)
    causal = ri >= ci
    strict = ri > ci
    eye = (ri == ci).astype(F32)

    for cc in range(cps):
        r = slice(cc * CHUNK, (cc + 1) * CHUNK)
        gc = _cumsum_rows(g_all[r])
        gct = jnp.concatenate([gc, gc], axis=0).T
        beta_c = beta_all[r]
        hs = range(DN_HEADS)
        q = jnp.stack([qs[r, h * DN_DK:(h + 1) * DN_DK] for h in hs], axis=0)
        k = jnp.stack([ks[r, h * DN_DK:(h + 1) * DN_DK] for h in hs], axis=0)
        v = jnp.stack([vs[r, h * DN_DK:(h + 1) * DN_DK] for h in hs], axis=0)
        beta = jnp.stack([beta_c[:, SM_BETA + h:SM_BETA + h + 1] for h in hs], axis=0)
        gcol = jnp.stack([gc[:, SM_A + h:SM_A + h + 1] for h in hs], axis=0)
        grow = jnp.stack([gct[SM_A + h:SM_A + h + 1, :CHUNK] for h in hs], axis=0)
        glast = gcol[:, CHUNK - 1:CHUNK, :]
        q = q * (lax.rsqrt(jnp.sum(q * q, axis=-1, keepdims=True) + NORM_EPS) * DN_DK ** -0.5)
        k = k * lax.rsqrt(jnp.sum(k * k, axis=-1, keepdims=True) + NORM_EPS)
        decay = jnp.exp(jnp.where(causal[None], gcol - grow, -jnp.inf))
        q16 = q.astype(BF16)
        k16 = k.astype(BF16)
        gram = _bdot_nt(jnp.concatenate([q16, k16], axis=1), k16)
        qkm = gram[:, :CHUNK] * decay
        a_mat = jnp.where(strict[None], beta * gram[:, CHUNK:] * decay, 0.0)
        x_inv = eye[None] - a_mat
        a16 = a_mat.astype(BF16)
        p = _bdot(a16, a16)
        for lvl in range(5):
            p16 = p.astype(BF16)
            if lvl < 4:
                xp = _bdot(jnp.concatenate([x_inv.astype(BF16), p16], axis=1), p16)
                x_inv = x_inv + xp[:, :CHUNK]
                p = xp[:, CHUNK:]
            else:
                x_inv = x_inv + _bdot(x_inv.astype(BF16), p16)
        eg = jnp.exp(gcol)
        rhs = jnp.concatenate([v * beta, k * (beta * eg)], axis=2).astype(BF16)
        wu = _bdot(x_inv.astype(BF16), rhs)
        u = wu[:, :, :DN_DK]
        w = wu[:, :, DN_DK:]
        s = s_sc[...]
        s16 = s.astype(BF16)
        v_new = u - _bdot(w.astype(BF16), s16)
        vn16 = v_new.astype(BF16)
        o = _bdot((q * eg).astype(BF16), s16) + _bdot(qkm.astype(BF16), vn16)
        kd = k * jnp.exp(glast - gcol)
        kdt16 = jnp.stack([kd[h].T for h in hs], axis=0).astype(BF16)
        s_sc[...] = s * jnp.exp(glast) + _bdot(kdt16, vn16)
        z = jnp.stack([z_ref[r, h * DN_DK:(h + 1) * DN_DK] for h in hs], axis=0).astype(F32)
        on = o * lax.rsqrt(jnp.mean(o * o, axis=-1, keepdims=True) + NORM_EPS) * nw * _silu(z)
        on = on.astype(o_ref.dtype)
        for h in hs:
            o_ref[r, h * DN_DK:(h + 1) * DN_DK] = on[h]

    @pl.when(c == pl.num_programs(1) - 1)
    def _():
        sfin_ref[0] = s_sc[...]


def gdn_scan(proj, small, conv_w, alog_row, dtb_row, norm_w, s0, tail, *, batch, cps):
    rows = proj.shape[0]
    rb = cps * CHUNK
    ncb = rows // batch // rb
    w = DN_HEADS * DN_DK

    def col(off):
        return lambda b, c: (b * ncb + c, off // w)

    kern = functools.partial(_gdn_kernel, cps=cps)
    return pl.pallas_call(
        kern,
        grid=(batch, ncb),
        in_specs=[
            pl.BlockSpec((rb, w), col(COL_Q)),
            pl.BlockSpec((rb, w), col(COL_K)),
            pl.BlockSpec((rb, w), col(COL_V)),
            pl.BlockSpec((rb, w), col(COL_DNZ)),
            pl.BlockSpec((rb, LANES), lambda b, c: (b * ncb + c, 0)),
            pl.BlockSpec((4, w), lambda b, c: (0, 0)),
            pl.BlockSpec((4, w), lambda b, c: (0, 1)),
            pl.BlockSpec((4, w), lambda b, c: (0, 2)),
            pl.BlockSpec((1, LANES), lambda b, c: (0, 0)),
            pl.BlockSpec((1, LANES), lambda b, c: (0, 0)),
            pl.BlockSpec((1, DN_DK), lambda b, c: (0, 0)),
            pl.BlockSpec((DN_HEADS, DN_DK, DN_DK), lambda b, c: (0, 0, 0)),
            pl.BlockSpec((TAIL, w), lambda b, c: (0, 0)),
            pl.BlockSpec((TAIL, w), lambda b, c: (0, 1)),
            pl.BlockSpec((TAIL, w), lambda b, c: (0, 2)),
        ],
        out_specs=[
            pl.BlockSpec((rb, w), lambda b, c: (b * ncb + c, 0)),
            pl.BlockSpec((1, DN_HEADS, DN_DK, DN_DK), lambda b, c: (b, 0, 0, 0)),
        ],
        out_shape=[
            jax.ShapeDtypeStruct((rows, w), BF16),
            jax.ShapeDtypeStruct((batch, DN_HEADS, DN_DK, DN_DK), F32),
        ],
        scratch_shapes=[
            pltpu.VMEM((DN_HEADS, DN_DK, DN_DK), F32),
            pltpu.VMEM((3, TAIL, w), F32),
        ],
        compiler_params=_cparams(2),
        name="gdn_scan",
    )(proj, proj, proj, proj, small, conv_w, conv_w, conv_w, alog_row, dtb_row, norm_w, s0,
      tail, tail, tail)


def _ssd_kernel(x_ref, b_ref, c_ref, z_ref, sm_ref, wx_ref, wb_ref, wc_ref, bx_ref, bb_ref, bc_ref,
                alog_ref, dtb_ref, d_ref, nw_ref, st0_ref, tx_ref, tb_ref, tc_ref,
                y_ref, stfin_ref, st_sc, tx_sc, tb_sc, tc_sc, *, cps, n_pad):
    c = pl.program_id(1)
    rb = cps * CHUNK

    @pl.when(c == 0)
    def _():
        st_sc[...] = st0_ref[...]
        tx_sc[...] = tx_ref[...]
        tb_sc[...] = tb_ref[...]
        tc_sc[...] = tc_ref[...]

    def conv_silu(x_ref, w_ref, bias_ref, t_sc):
        x = x_ref[...].astype(F32)
        y = _causal_conv(x, t_sc[...], w_ref[...], 4) + bias_ref[...]
        t_sc[...] = x[rb - TAIL:rb]
        return _silu(y)

    xs = conv_silu(x_ref, wx_ref, bx_ref, tx_sc)
    bm = conv_silu(b_ref, wb_ref, bb_ref, tb_sc)
    cm = conv_silu(c_ref, wc_ref, bc_ref, tc_sc)

    sm = sm_ref[...]
    dt_all = _softplus(sm + dtb_ref[...])
    a_all = dt_all * (-jnp.exp(alog_ref[...]))
    d_row = d_ref[...]
    nw = nw_ref[...]

    lane = lax.broadcasted_iota(jnp.int32, (CHUNK, LANES), 1)
    rowi = lax.broadcasted_iota(jnp.int32, (CHUNK, LANES), 0)
    lo = lane < M2_P
    causal2 = rowi >= jnp.where(lo, lane, lane - M2_P)
    lo2 = lax.broadcasted_iota(jnp.int32, (2 * CHUNK, LANES), 1) < M2_P
    top2 = lax.broadcasted_iota(jnp.int32, (2 * CHUNK, LANES), 0) < CHUNK
    blk2 = lo2 == top2

    npair = M2_HEADS // 2
    ppg = npair // M2_GROUPS

    for cc in range(cps):
        r = slice(cc * CHUNK, (cc + 1) * CHUNK)
        acs = _cumsum_rows(a_all[r])
        zt = jnp.concatenate([acs, pltpu.roll(acs, LANES - 1, axis=1)], axis=0).T
        dt = dt_all[r]
        y_tiles = []
        for g in range(M2_GROUPS):
            gs = slice(g * M2_STATE, (g + 1) * M2_STATE)
            b16 = bm[r, gs].astype(BF16)
            c16 = cm[r, gs].astype(BF16)
            cbcb = _dot_nt(c16, jnp.concatenate([b16, b16], axis=0))
            st = st_sc[g]
            yoff = _dot(c16, st.astype(BF16))
            xdec_tiles = []
            ea_last = []
            for mm in range(ppg):
                m = g * ppg + mm
                la = SM_DT + 2 * m
                ea = jnp.where(lo, acs[:, la:la + 1], acs[:, la + 1:la + 2])
                edt = jnp.where(lo, dt[:, la:la + 1], dt[:, la + 1:la + 2])
                xs_m = xs[r, m * LANES:(m + 1) * LANES]
                xdt = xs_m * edt
                if n_pad:
                    xdt = jnp.where(rowi + cc * CHUNK >= n_pad, xdt, 0.0)
                rowp = zt[la:la + 1, :]
                lmat = jnp.exp(jnp.where(causal2, ea - rowp, -jnp.inf))
                wp = (cbcb * lmat).astype(BF16)
                x16 = xdt.astype(BF16)
                x2 = jnp.concatenate([x16, x16], axis=0)
                rhs = jnp.where(blk2, x2, jnp.zeros_like(x2))
                y = _dot(wp, rhs) + yoff[:, mm * LANES:(mm + 1) * LANES] * jnp.exp(ea)
                y_tiles.append(y + d_row[:, m * LANES:(m + 1) * LANES] * xs_m)
                eal = ea[CHUNK - 1:CHUNK]
                ea_last.append(eal)
                xdec_tiles.append((xdt * jnp.exp(eal - ea)).astype(BF16))
            xdec = jnp.concatenate(xdec_tiles, axis=1)
            sdec = jnp.exp(jnp.concatenate(ea_last, axis=1))
            bt16 = bm[r, gs].T.astype(BF16)
            st_sc[g] = st * sdec + _dot(bt16, xdec)
        for g in range(M2_GROUPS):
            tiles = []
            ssq = None
            for mm in range(ppg):
                m = g * ppg + mm
                z = z_ref[r, m * LANES:(m + 1) * LANES].astype(F32)
                t = y_tiles[m] * _silu(z)
                tiles.append(t)
                part = jnp.sum(t * t, axis=-1, keepdims=True)
                ssq = part if ssq is None else ssq + part
            scale = lax.rsqrt(ssq * (1.0 / M2_GW) + NORM_EPS)
            for mm in range(ppg):
                m = g * ppg + mm
                cs = slice(m * LANES, (m + 1) * LANES)
                y_ref[r, cs] = (tiles[mm] * scale * nw[:, cs]).astype(y_ref.dtype)

    @pl.when(c == pl.num_programs(1) - 1)
    def _():
        stfin_ref[0] = st_sc[...]


def ssd_scan(proj, small, conv_w, conv_b, alog_row, dtb_row, d_row, norm_w, st0, tail, *,
             batch, cps, n_pad):
    rows = proj.shape[0]
    rb = cps * CHUNK
    ncb = rows // batch // rb
    wx = M2_HEADS * M2_P
    wg = M2_GROUPS * M2_STATE

    def rowblk(width, off):
        return pl.BlockSpec((rb, width), lambda b, c: (b * ncb + c, off // width))

    def const(shape, *idx):
        return pl.BlockSpec(shape, lambda b, c: idx)

    kern = functools.partial(_ssd_kernel, cps=cps, n_pad=n_pad)
    return pl.pallas_call(
        kern,
        grid=(batch, ncb),
        in_specs=[
            rowblk(wx, COL_XS),
            rowblk(wg, COL_B),
            rowblk(wg, COL_C),
            rowblk(wx, COL_M2Z),
            pl.BlockSpec((rb, LANES), lambda b, c: (b * ncb + c, 0)),
            const((4, wx), 0, 0),
            const((4, wg), 0, wx // wg),
            const((4, wg), 0, wx // wg + 1),
            const((1, wx), 0, 0),
            const((1, wg), 0, wx // wg),
            const((1, wg), 0, wx // wg + 1),
            const((1, LANES), 0, 0),
            const((1, LANES), 0, 0),
            const((1, wx), 0, 0),
            const((1, wx), 0, 0),
            const((M2_GROUPS, M2_STATE, M2_GW), 0, 0, 0),
            const((TAIL, wx), 0, 0),
            const((TAIL, wg), 0, wx // wg),
            const((TAIL, wg), 0, wx // wg + 1),
        ],
        out_specs=[
            pl.BlockSpec((rb, wx), lambda b, c: (b * ncb + c, 0)),
            pl.BlockSpec((1, M2_GROUPS, M2_STATE, M2_GW), lambda b, c: (b, 0, 0, 0)),
        ],
        out_shape=[
            jax.ShapeDtypeStruct((rows, wx), BF16),
            jax.ShapeDtypeStruct((batch, M2_GROUPS, M2_STATE, M2_GW), F32),
        ],
        scratch_shapes=[
            pltpu.VMEM((M2_GROUPS, M2_STATE, M2_GW), F32),
            pltpu.VMEM((TAIL, wx), F32),
            pltpu.VMEM((TAIL, wg), F32),
            pltpu.VMEM((TAIL, wg), F32),
        ],
        compiler_params=_cparams(2),
        name="ssd_scan",
    )(proj, proj, proj, proj, small, conv_w, conv_w, conv_w, conv_b, conv_b, conv_b,
      alog_row, dtb_row, d_row, norm_w, st0, tail, tail, tail)


def _out_proj_kernel(x_ref, o_ref, y_ref, wo_ref, wy_ref, h_ref):
    h_ref[...] = x_ref[...] + _dot(o_ref[...], wo_ref[...]) + _dot(y_ref[...], wy_ref[...])


def out_proj(x, o, y, w_out, *, tm, tn):
    rows, d = x.shape
    km = o.shape[1]
    return pl.pallas_call(
        _out_proj_kernel,
        grid=(d // tn, rows // tm),
        in_specs=[
            pl.BlockSpec((tm, tn), lambda j, i: (i, j)),
            pl.BlockSpec((tm, km), lambda j, i: (i, 0)),
            pl.BlockSpec((tm, km), lambda j, i: (i, 0)),
            pl.BlockSpec((km, tn), lambda j, i: (0, j)),
            pl.BlockSpec((km, tn), lambda j, i: (1, j)),
        ],
        out_specs=pl.BlockSpec((tm, tn), lambda j, i: (i, j)),
        out_shape=jax.ShapeDtypeStruct((rows, d), F32),
        compiler_params=_cparams(2),
        name="out_proj",
    )(x, o, y, w_out, w_out)


def _conv_ffn_kernel(h_ref, gn_ref, wg_ref, wv_ref, cwg_ref, cwv_ref, wd_ref, tg_ref, tv_ref,
                     fn_ref, o_ref, hn_sc, acc_sc, cg_sc, cv_sc, *, tiles_per_batch):
    i = pl.program_id(0)
    f = pl.program_id(1)
    nf = pl.num_programs(1)
    tm = h_ref.shape[0]

    @pl.when(f == 0)
    def _():
        h = h_ref[...]
        ms = jnp.mean(h * h, axis=-1, keepdims=True)
        hn_sc[...] = (h * lax.rsqrt(ms + NORM_EPS) * gn_ref[...]).astype(BF16)

    first = (i % tiles_per_batch) == 0
    hn = hn_sc[...]

    def branch(w_ref, cw_ref, t_ref, c_sc):
        u = _dot(hn, w_ref[...])
        tail = jnp.where(first, t_ref[...], c_sc[f])
        y = _causal_conv(u, tail, cw_ref[...], 3)
        c_sc[f] = u[tm - TAIL:tm]
        return y

    gate = branch(wg_ref, cwg_ref, tg_ref, cg_sc)
    val = branch(wv_ref, cwv_ref, tv_ref, cv_sc)
    act = (_silu(gate) * val).astype(BF16)
    contrib = _dot(act, wd_ref[...])

    @pl.when(f == 0)
    def _():
        acc_sc[...] = contrib

    @pl.when(f > 0)
    def _():
        acc_sc[...] += contrib

    @pl.when(f == nf - 1)
    def _():
        h2 = h_ref[...] + acc_sc[...]
        ms = jnp.mean(h2 * h2, axis=-1, keepdims=True)
        o_ref[...] = h2 * lax.rsqrt(ms + NORM_EPS) * fn_ref[...]


def conv_ffn(h, gain, w_up, conv_w, w_down, u_tail, final_gain, *, tm, tf, tiles_per_batch):
    rows, d = h.shape
    dff = w_down.shape[0]
    nf = dff // tf
    kern = functools.partial(_conv_ffn_kernel, tiles_per_batch=tiles_per_batch)
    return pl.pallas_call(
        kern,
        grid=(rows // tm, nf),
        in_specs=[
            pl.BlockSpec((tm, d), lambda i, f: (i, 0)),
            pl.BlockSpec((1, d), lambda i, f: (0, 0)),
            pl.BlockSpec((d, tf), lambda i, f: (0, f)),
            pl.BlockSpec((d, tf), lambda i, f: (0, nf + f)),
            pl.BlockSpec((3, tf), lambda i, f: (0, f)),
            pl.BlockSpec((3, tf), lambda i, f: (0, nf + f)),
            pl.BlockSpec((tf, d), lambda i, f: (f, 0)),
            pl.BlockSpec((TAIL, tf), lambda i, f: (0, f)),
            pl.BlockSpec((TAIL, tf), lambda i, f: (0, nf + f)),
            pl.BlockSpec((1, d), lambda i, f: (0, 0)),
        ],
        out_specs=pl.BlockSpec((tm, d), lambda i, f: (i, 0)),
        out_shape=jax.ShapeDtypeStruct((rows, d), F32),
        scratch_shapes=[
            pltpu.VMEM((tm, d), BF16),
            pltpu.VMEM((tm, d), F32),
            pltpu.VMEM((nf, TAIL, tf), F32),
            pltpu.VMEM((nf, TAIL, tf), F32),
        ],
        compiler_params=_cparams(2),
        name="conv_ffn",
    )(h, gain, w_up, w_up, conv_w, conv_w, w_down, u_tail, u_tail, final_gain)


def _row_tile(rows, pref):
    t = min(rows, pref)
    while rows % t:
        t //= 2
    return t


def _lane_row(vec, offset):
    return jnp.zeros((1, LANES), F32).at[0, offset:offset + vec.shape[0]].set(vec.astype(F32))


def kernel(x, meta_tokens, norm_mix_w, w_in, dn_conv_w, dn_a_log, dn_dt_bias, dn_norm_w, m2_conv_w, m2_conv_b, m2_a_log, m2_dt_bias, m2_d, m2_norm_w, w_out, norm_ffn_w, ffn_up, ffn_conv_w, ffn_down, norm_final_w):
    batch, seq, d = x.shape
    depth = w_in.shape[0]
    assert depth == 1 and d == D_MODEL and seq % CHUNK == 0
    rows = batch * seq
    pad = CHUNK - N_META

    w = w_in[0]
    o_qkv, o_z, o_b, o_a = 0, 6144, 8192, 8208
    o_m2z, o_xbc, o_dt = 8224, 10272, 13344
    w_big = jnp.concatenate(
        [w[:, o_qkv:o_z], w[:, o_z:o_b], w[:, o_m2z:o_xbc], w[:, o_xbc:o_dt]], axis=1).astype(BF16)
    w_small = jnp.concatenate(
        [w[:, o_b:o_a], w[:, o_a:o_m2z], w[:, o_dt:], jnp.zeros((d, LANES - 64), w.dtype)],
        axis=1).astype(BF16)
    w_out16 = w_out[0].astype(BF16)
    ffn_up16 = ffn_up[0].astype(BF16)
    ffn_down16 = ffn_down[0].astype(BF16)
    dff = ffn_down16.shape[0]
    zero_small = jnp.zeros((d, LANES), BF16)

    gain_mix = norm_mix_w[0].reshape(1, d).astype(F32)
    gain_ffn = norm_ffn_w[0].reshape(1, d).astype(F32)
    gain_fin = norm_final_w.reshape(1, d).astype(F32)
    dn_alog_row = _lane_row(dn_a_log[0], SM_A)
    dn_dtb_row = _lane_row(dn_dt_bias[0], SM_A)
    m2_alog_row = _lane_row(m2_a_log[0], SM_DT)
    m2_dtb_row = _lane_row(m2_dt_bias[0], SM_DT)
    dn_cw = dn_conv_w[0].astype(F32)
    dn_nw = dn_norm_w[0].reshape(1, DN_DK).astype(F32)
    m2_cw = m2_conv_w[0].astype(F32)
    m2_cb = m2_conv_b[0].reshape(1, -1).astype(F32)
    m2_d_row = jnp.repeat(m2_d[0].astype(F32), M2_P).reshape(1, M2_HEADS * M2_P)
    m2_nw = m2_norm_w[0].reshape(1, -1).astype(F32)
    ffn_cw = ffn_conv_w[0].astype(F32)

    hm = jnp.concatenate([jnp.zeros((pad, d), F32), meta_tokens.astype(F32)], axis=0)
    pm_big, pm_small = norm_matmul(hm, gain_mix, w_big, w_small, tm=CHUNK, tn=1024, out_dtype=BF16)
    zeros_tail = jnp.zeros((TAIL, N_BIG), F32)
    o_m, s_meta = gdn_scan(pm_big, pm_small, dn_cw, dn_alog_row, dn_dtb_row, dn_nw,
                           jnp.zeros((DN_HEADS, DN_DK, DN_DK), F32), zeros_tail[:, :3 * 2048],
                           batch=1, cps=1)
    y_m, st_meta = ssd_scan(pm_big, pm_small, m2_cw, m2_cb, m2_alog_row, m2_dtb_row, m2_d_row, m2_nw,
                            jnp.zeros((M2_GROUPS, M2_STATE, M2_GW), F32),
                            zeros_tail[:, :3072], batch=1, cps=1, n_pad=pad)
    h1_m = out_proj(hm, o_m, y_m, w_out16, tm=CHUNK, tn=1024)
    u_m, _ = norm_matmul(h1_m, gain_ffn, ffn_up16, zero_small, tm=CHUNK, tn=1024, out_dtype=F32)
    tail_big = pm_big[CHUNK - TAIL:].astype(F32)
    tail_qkv = tail_big[:, COL_Q:COL_DNZ]
    tail_xbc = tail_big[:, COL_XS:N_BIG]
    tail_u = u_m[CHUNK - TAIL:]

    x2 = x.reshape(rows, d).astype(F32)
    p_big, p_small = norm_matmul(x2, gain_mix, w_big, w_small, tm=_row_tile(rows, 1024), tn=1024,
                                 out_dtype=BF16)
    o_g, _ = gdn_scan(p_big, p_small, dn_cw, dn_alog_row, dn_dtb_row, dn_nw, s_meta[0], tail_qkv,
                      batch=batch, cps=1)
    y_g, _ = ssd_scan(p_big, p_small, m2_cw, m2_cb, m2_alog_row, m2_dtb_row, m2_d_row, m2_nw,
                      st_meta[0], tail_xbc, batch=batch, cps=1, n_pad=0)
    h1 = out_proj(x2, o_g, y_g, w_out16, tm=_row_tile(rows, 512), tn=1024)
    tm_ffn = _row_tile(seq, 512)
    out = conv_ffn(h1, gain_ffn, ffn_up16, ffn_cw, ffn_down16, tail_u, gain_fin,
                   tm=tm_ffn, tf=512, tiles_per_batch=seq // tm_ffn)
    return out.reshape(batch, seq, d)
```

```python
import functools

import jax
import jax.numpy as jnp
from jax import lax
from jax.experimental import pallas as pl
from jax.experimental.pallas import tpu as pltpu

F32 = jnp.float32
BF16 = jnp.bfloat16

NORM_EPS = 1e-6
N_META = 16
CHUNK = 64
LANES = 128
TAIL = 8
TAIL16 = 16

DN_HEADS = 16
DN_DK = 128
M2_HEADS = 32
M2_P = 64
M2_GROUPS = 4
M2_STATE = 128
M2_GW = M2_HEADS * M2_P // M2_GROUPS

D_MODEL = 2048
COL_Q, COL_K, COL_V = 0, 2048, 4096
COL_DNZ = 6144
COL_M2Z = 8192
COL_XS = 10240
COL_B = 12288
COL_C = 12800
N_BIG = 13312
SM_BETA, SM_A, SM_DT = 0, 16, 32

VMEM_LIMIT = 56 * 1024 * 1024

GDN_CPS = 2
SSD_CPS = 2


def _cparams(n_axes):
    return pltpu.CompilerParams(dimension_semantics=("arbitrary",) * n_axes,
                                vmem_limit_bytes=VMEM_LIMIT)


def _dot(a, b):
    return jnp.dot(a, b, preferred_element_type=F32)


def _dot_nt(a, b):
    return lax.dot_general(a, b, (((1,), (1,)), ((), ())), preferred_element_type=F32)


def _bdot(a, b):
    return lax.dot_general(a, b, (((2,), (1,)), ((0,), (0,))), preferred_element_type=F32)


def _bdot_nt(a, b):
    return lax.dot_general(a, b, (((2,), (2,)), ((0,), (0,))), preferred_element_type=F32)


def _sigmoid(x):
    return 1.0 / (1.0 + jnp.exp(-x))


def _silu(x):
    return x * _sigmoid(x)


def _softplus(x):
    return jnp.maximum(x, 0.0) + jnp.log1p(jnp.exp(-jnp.abs(x)))


def _cumsum_rows(x):
    n = x.shape[0]
    row = lax.broadcasted_iota(jnp.int32, x.shape, 0)
    s = 1
    while s < n:
        x = x + jnp.where(row >= s, pltpu.roll(x, s, axis=0), 0.0)
        s *= 2
    return x


def _causal_conv(xe_ref, start, n, w, k):
    acc = None
    for j in range(k):
        term = xe_ref[pl.ds(start - (k - 1) + j, n), :] * w[j:j + 1]
        acc = term if acc is None else acc + term
    return acc


def _causal_conv_val(x, tail, w, k):
    xe = jnp.concatenate([tail, x], axis=0)
    acc = x * w[k - 1:k]
    for j in range(k - 1):
        acc = acc + pltpu.roll(xe, k - 1 - j, axis=0)[TAIL:] * w[j:j + 1]
    return acc


def _shift_matrix(n, k):
    r = lax.broadcasted_iota(jnp.int32, ((k - 1) * n, TAIL16 + n), 0)
    c = lax.broadcasted_iota(jnp.int32, ((k - 1) * n, TAIL16 + n), 1)
    j = r // n
    t = r - j * n
    return (c == t + TAIL16 - (k - 1) + j).astype(BF16)


def _causal_conv_bf16(x16, tail16, w, shift):
    n = x16.shape[0]
    k = w.shape[0]
    sh = _dot(shift, jnp.concatenate([tail16, x16], axis=0))
    acc = x16.astype(F32) * w[k - 1:k]
    for j in range(k - 1):
        acc = acc + sh[j * n:(j + 1) * n] * w[j:j + 1]
    return acc


def _norm_matmul_kernel(x_ref, g_ref, w_ref, ws_ref, o_ref, os_ref, xn_ref):
    @pl.when(pl.program_id(1) == 0)
    def _():
        x = x_ref[...]
        ms = jnp.mean(x * x, axis=-1, keepdims=True)
        xn = (x * lax.rsqrt(ms + NORM_EPS) * g_ref[...]).astype(BF16)
        xn_ref[...] = xn
        os_ref[...] = _dot(xn, ws_ref[...])

    o_ref[...] = _dot(xn_ref[...], w_ref[...]).astype(o_ref.dtype)


def norm_matmul(x, gain, w, w_small, *, tm, tn, out_dtype):
    rows, k = x.shape
    n = w.shape[1]
    ns = w_small.shape[1]
    return pl.pallas_call(
        _norm_matmul_kernel,
        grid=(rows // tm, n // tn),
        in_specs=[
            pl.BlockSpec((tm, k), lambda i, j: (i, 0)),
            pl.BlockSpec((1, k), lambda i, j: (0, 0)),
            pl.BlockSpec((k, tn), lambda i, j: (0, j)),
            pl.BlockSpec((k, ns), lambda i, j: (0, 0)),
        ],
        out_specs=[
            pl.BlockSpec((tm, tn), lambda i, j: (i, j)),
            pl.BlockSpec((tm, ns), lambda i, j: (i, 0)),
        ],
        out_shape=[
            jax.ShapeDtypeStruct((rows, n), out_dtype),
            jax.ShapeDtypeStruct((rows, ns), F32),
        ],
        scratch_shapes=[pltpu.VMEM((tm, k), BF16)],
        compiler_params=_cparams(2),
        name="norm_matmul",
    )(x, gain, w, w_small)


def _gdn_kernel(q_ref, k_ref, v_ref, z_ref, sm_ref, wq_ref, wk_ref, wv_ref, alog_ref, dtb_ref,
                nw_ref, s0_ref, tq_ref, tk_ref, tv_ref,
                o_ref, sfin_ref, s_sc, tail_sc, *, cps):
    c = pl.program_id(1)
    rb = cps * CHUNK

    @pl.when(c == 0)
    def _():
        s_sc[...] = s0_ref[...]
        tail_sc[0] = tq_ref[...]
        tail_sc[1] = tk_ref[...]
        tail_sc[2] = tv_ref[...]

    shift = _shift_matrix(rb, 4)

    def conv_silu(x_ref, w_ref, idx):
        x16 = x_ref[...]
        y = _causal_conv_bf16(x16, tail_sc[idx], w_ref[...], shift)
        tail_sc[idx] = x16[rb - TAIL16:rb]
        return _silu(y)

    qs = conv_silu(q_ref, wq_ref, 0)
    ks = conv_silu(k_ref, wk_ref, 1)
    vs = conv_silu(v_ref, wv_ref, 2)

    sm = sm_ref[...]
    beta_all = _sigmoid(sm)
    g_all = -jnp.exp(alog_ref[...]) * _softplus(sm + dtb_ref[...])
    nw = nw_ref[...]

    ri = lax.broadcasted_iota(jnp.int32, (CHUNK, CHUNK), 0)
    ci = lax.broadcasted_iota(jnp.int32, (CHUNK, CHUNK), 1)
    causal = ri >= ci
    strict = ri > ci
    eye = (ri == ci).astype(F32)

    for cc in range(cps):
        r = slice(cc * CHUNK, (cc + 1) * CHUNK)
        gc = _cumsum_rows(g_all[r])
        gct = jnp.concatenate([gc, gc], axis=0).T
        beta_c = beta_all[r]
        hs = range(DN_HEADS)
        q = jnp.stack([qs[r, h * DN_DK:(h + 1) * DN_DK] for h in hs], axis=0)
        k = jnp.stack([ks[r, h * DN_DK:(h + 1) * DN_DK] for h in hs], axis=0)
        v = jnp.stack([vs[r, h * DN_DK:(h + 1) * DN_DK] for h in hs], axis=0)
        beta = jnp.stack([beta_c[:, SM_BETA + h:SM_BETA + h + 1] for h in hs], axis=0)
        gcol = jnp.stack([gc[:, SM_A + h:SM_A + h + 1] for h in hs], axis=0)
        grow = jnp.stack([gct[SM_A + h:SM_A + h + 1, :CHUNK] for h in hs], axis=0)
        glast = gcol[:, CHUNK - 1:CHUNK, :]
        q = q * (lax.rsqrt(jnp.sum(q * q, axis=-1, keepdims=True) + NORM_EPS) * DN_DK ** -0.5)
        k = k * lax.rsqrt(jnp.sum(k * k, axis=-1, keepdims=True) + NORM_EPS)
        decay = jnp.exp(jnp.where(causal[None], gcol - grow, -jnp.inf))
        q16 = q.astype(BF16)
        k16 = k.astype(BF16)
        gram = _bdot_nt(jnp.concatenate([q16, k16], axis=1), k16)
        qkm = gram[:, :CHUNK] * decay
        a_mat = jnp.where(strict[None], beta * gram[:, CHUNK:] * decay, 0.0)
        x_inv = eye[None] - a_mat
        a16 = a_mat.astype(BF16)
        p = _bdot(a16, a16)
        for lvl in range(5):
            p16 = p.astype(BF16)
            if lvl < 4:
                xp = _bdot(jnp.concatenate([x_inv.astype(BF16), p16], axis=1), p16)
                x_inv = x_inv + xp[:, :CHUNK]
                p = xp[:, CHUNK:]
            else:
                x_inv = x_inv + _bdot(x_inv.astype(BF16), p16)
        eg = jnp.exp(gcol)
        rhs = jnp.concatenate([v * beta, k * (beta * eg)], axis=2).astype(BF16)
        wu = _bdot(x_inv.astype(BF16), rhs)
        u = wu[:, :, :DN_DK]
        w = wu[:, :, DN_DK:]
        s = s_sc[...]
        s16 = s.astype(BF16)
        v_new = u - _bdot(w.astype(BF16), s16)
        vn16 = v_new.astype(BF16)
        o = _bdot((q * eg).astype(BF16), s16) + _bdot(qkm.astype(BF16), vn16)
        kd = k * jnp.exp(glast - gcol)
        kdt16 = jnp.stack([kd[h].T for h in hs], axis=0).astype(BF16)
        s_sc[...] = s * jnp.exp(glast) + _bdot(kdt16, vn16)
        z = jnp.stack([z_ref[r, h * DN_DK:(h + 1) * DN_DK] for h in hs], axis=0).astype(F32)
        on = o * lax.rsqrt(jnp.mean(o * o, axis=-1, keepdims=True) + NORM_EPS) * nw * _silu(z)
        on = on.astype(o_ref.dtype)
        for h in hs:
            o_ref[r, h * DN_DK:(h + 1) * DN_DK] = on[h]

    @pl.when(c == pl.num_programs(1) - 1)
    def _():
        sfin_ref[0] = s_sc[...]


def gdn_scan(proj, small, conv_w, alog_row, dtb_row, norm_w, s0, tail, *, batch, cps):
    rows = proj.shape[0]
    rb = cps * CHUNK
    ncb = rows // batch // rb
    w = DN_HEADS * DN_DK

    def col(off):
        return lambda b, c: (b * ncb + c, off // w)

    kern = functools.partial(_gdn_kernel, cps=cps)
    return pl.pallas_call(
        kern,
        grid=(batch, ncb),
        in_specs=[
            pl.BlockSpec((rb, w), col(COL_Q)),
            pl.BlockSpec((rb, w), col(COL_K)),
            pl.BlockSpec((rb, w), col(COL_V)),
            pl.BlockSpec((rb, w), col(COL_DNZ)),
            pl.BlockSpec((rb, LANES), lambda b, c: (b * ncb + c, 0)),
            pl.BlockSpec((4, w), lambda b, c: (0, 0)),
            pl.BlockSpec((4, w), lambda b, c: (0, 1)),
            pl.BlockSpec((4, w), lambda b, c: (0, 2)),
            pl.BlockSpec((1, LANES), lambda b, c: (0, 0)),
            pl.BlockSpec((1, LANES), lambda b, c: (0, 0)),
            pl.BlockSpec((1, DN_DK), lambda b, c: (0, 0)),
            pl.BlockSpec((DN_HEADS, DN_DK, DN_DK), lambda b, c: (0, 0, 0)),
            pl.BlockSpec((TAIL16, w), lambda b, c: (0, 0)),
            pl.BlockSpec((TAIL16, w), lambda b, c: (0, 1)),
            pl.BlockSpec((TAIL16, w), lambda b, c: (0, 2)),
        ],
        out_specs=[
            pl.BlockSpec((rb, w), lambda b, c: (b * ncb + c, 0)),
            pl.BlockSpec((1, DN_HEADS, DN_DK, DN_DK), lambda b, c: (b, 0, 0, 0)),
        ],
        out_shape=[
            jax.ShapeDtypeStruct((rows, w), BF16),
            jax.ShapeDtypeStruct((batch, DN_HEADS, DN_DK, DN_DK), F32),
        ],
        scratch_shapes=[
            pltpu.VMEM((DN_HEADS, DN_DK, DN_DK), F32),
            pltpu.VMEM((3, TAIL16, w), BF16),
        ],
        compiler_params=_cparams(2),
        name="gdn_scan",
    )(proj, proj, proj, proj, small, conv_w, conv_w, conv_w, alog_row, dtb_row, norm_w, s0,
      tail, tail, tail)


def _ssd_kernel(x_ref, b_ref, c_ref, z_ref, sm_ref, wx_ref, wb_ref, wc_ref, bx_ref, bb_ref, bc_ref,
                alog_ref, dtb_ref, d_ref, nw_ref, st0_ref, tx_ref, tb_ref, tc_ref,
                y_ref, stfin_ref, st_sc, tx_sc, tb_sc, tc_sc, *, cps, n_pad):
    c = pl.program_id(1)
    rb = cps * CHUNK

    @pl.when(c == 0)
    def _():
        st_sc[...] = st0_ref[...]
        tx_sc[...] = tx_ref[...]
        tb_sc[...] = tb_ref[...]
        tc_sc[...] = tc_ref[...]

    shift = _shift_matrix(rb, 4)

    def conv_silu(x_ref, w_ref, bias_ref, t_sc):
        x16 = x_ref[...]
        y = _causal_conv_bf16(x16, t_sc[...], w_ref[...], shift) + bias_ref[...]
        t_sc[...] = x16[rb - TAIL16:rb]
        return _silu(y)

    def conv_silu_f32(x_ref, w_ref, bias_ref, t_sc):
        x = x_ref[...].astype(F32)
        y = _causal_conv_val(x, t_sc[...], w_ref[...], 4) + bias_ref[...]
        t_sc[...] = x[rb - TAIL:rb]
        return _silu(y)

    xs = conv_silu(x_ref, wx_ref, bx_ref, tx_sc)
    bm = conv_silu_f32(b_ref, wb_ref, bb_ref, tb_sc)
    cm = conv_silu_f32(c_ref, wc_ref, bc_ref, tc_sc)

    sm = sm_ref[...]
    dt_all = _softplus(sm + dtb_ref[...])
    a_all = dt_all * (-jnp.exp(alog_ref[...]))
    d_row = d_ref[...]
    nw = nw_ref[...]

    lane = lax.broadcasted_iota(jnp.int32, (CHUNK, LANES), 1)
    rowi = lax.broadcasted_iota(jnp.int32, (CHUNK, LANES), 0)
    lo = lane < M2_P
    causal2 = rowi >= jnp.where(lo, lane, lane - M2_P)
    lo2 = lax.broadcasted_iota(jnp.int32, (2 * CHUNK, LANES), 1) < M2_P
    top2 = lax.broadcasted_iota(jnp.int32, (2 * CHUNK, LANES), 0) < CHUNK
    blk2 = lo2 == top2

    npair = M2_HEADS // 2
    ppg = npair // M2_GROUPS

    for cc in range(cps):
        r = slice(cc * CHUNK, (cc + 1) * CHUNK)
        acs = _cumsum_rows(a_all[r])
        zt = jnp.concatenate([acs, pltpu.roll(acs, LANES - 1, axis=1)], axis=0).T
        dt = dt_all[r]
        y_tiles = []
        for g in range(M2_GROUPS):
            gs = slice(g * M2_STATE, (g + 1) * M2_STATE)
            b16 = bm[r, gs].astype(BF16)
            c16 = cm[r, gs].astype(BF16)
            cbcb = _dot_nt(c16, jnp.concatenate([b16, b16], axis=0))
            st = st_sc[g]
            yoff = _dot(c16, st.astype(BF16))
            xdec_tiles = []
            ea_last = []
            for mm in range(ppg):
                m = g * ppg + mm
                la = SM_DT + 2 * m
                ea = jnp.where(lo, acs[:, la:la + 1], acs[:, la + 1:la + 2])
                edt = jnp.where(lo, dt[:, la:la + 1], dt[:, la + 1:la + 2])
                xs_m = xs[r, m * LANES:(m + 1) * LANES]
                xdt = xs_m * edt
                if n_pad:
                    xdt = jnp.where(rowi + cc * CHUNK >= n_pad, xdt, 0.0)
                rowp = zt[la:la + 1, :]
                lmat = jnp.exp(jnp.where(causal2, ea - rowp, -jnp.inf))
                wp = (cbcb * lmat).astype(BF16)
                x16 = xdt.astype(BF16)
                x2 = jnp.concatenate([x16, x16], axis=0)
                rhs = jnp.where(blk2, x2, jnp.zeros_like(x2))
                y = _dot(wp, rhs) + yoff[:, mm * LANES:(mm + 1) * LANES] * jnp.exp(ea)
                y_tiles.append(y + d_row[:, m * LANES:(m + 1) * LANES] * xs_m)
                eal = ea[CHUNK - 1:CHUNK]
                ea_last.append(eal)
                xdec_tiles.append((xdt * jnp.exp(eal - ea)).astype(BF16))
            xdec = jnp.concatenate(xdec_tiles, axis=1)
            sdec = jnp.exp(jnp.concatenate(ea_last, axis=1))
            bt16 = bm[r, gs].T.astype(BF16)
            st_sc[g] = st * sdec + _dot(bt16, xdec)
        for g in range(M2_GROUPS):
            tiles = []
            ssq = None
            for mm in range(ppg):
                m = g * ppg + mm
                z = z_ref[r, m * LANES:(m + 1) * LANES].astype(F32)
                t = y_tiles[m] * _silu(z)
                tiles.append(t)
                part = jnp.sum(t * t, axis=-1, keepdims=True)
                ssq = part if ssq is None else ssq + part
            scale = lax.rsqrt(ssq * (1.0 / M2_GW) + NORM_EPS)
            for mm in range(ppg):
                m = g * ppg + mm
                cs = slice(m * LANES, (m + 1) * LANES)
                y_ref[r, cs] = (tiles[mm] * scale * nw[:, cs]).astype(y_ref.dtype)

    @pl.when(c == pl.num_programs(1) - 1)
    def _():
        stfin_ref[0] = st_sc[...]


def ssd_scan(proj, small, conv_w, conv_b, alog_row, dtb_row, d_row, norm_w, st0, tail_x, tail_bc, *,
             batch, cps, n_pad):
    rows = proj.shape[0]
    rb = cps * CHUNK
    ncb = rows // batch // rb
    wx = M2_HEADS * M2_P
    wg = M2_GROUPS * M2_STATE

    def rowblk(width, off):
        return pl.BlockSpec((rb, width), lambda b, c: (b * ncb + c, off // width))

    def const(shape, *idx):
        return pl.BlockSpec(shape, lambda b, c: idx)

    kern = functools.partial(_ssd_kernel, cps=cps, n_pad=n_pad)
    return pl.pallas_call(
        kern,
        grid=(batch, ncb),
        in_specs=[
            rowblk(wx, COL_XS),
            rowblk(wg, COL_B),
            rowblk(wg, COL_C),
            rowblk(wx, COL_M2Z),
            pl.BlockSpec((rb, LANES), lambda b, c: (b * ncb + c, 0)),
            const((4, wx), 0, 0),
            const((4, wg), 0, wx // wg),
            const((4, wg), 0, wx // wg + 1),
            const((1, wx), 0, 0),
            const((1, wg), 0, wx // wg),
            const((1, wg), 0, wx // wg + 1),
            const((1, LANES), 0, 0),
            const((1, LANES), 0, 0),
            const((1, wx), 0, 0),
            const((1, wx), 0, 0),
            const((M2_GROUPS, M2_STATE, M2_GW), 0, 0, 0),
            const((TAIL16, wx), 0, 0),
            const((TAIL, wg), 0, 0),
            const((TAIL, wg), 0, 1),
        ],
        out_specs=[
            pl.BlockSpec((rb, wx), lambda b, c: (b * ncb + c, 0)),
            pl.BlockSpec((1, M2_GROUPS, M2_STATE, M2_GW), lambda b, c: (b, 0, 0, 0)),
        ],
        out_shape=[
            jax.ShapeDtypeStruct((rows, wx), BF16),
            jax.ShapeDtypeStruct((batch, M2_GROUPS, M2_STATE, M2_GW), F32),
        ],
        scratch_shapes=[
            pltpu.VMEM((M2_GROUPS, M2_STATE, M2_GW), F32),
            pltpu.VMEM((TAIL16, wx), BF16),
            pltpu.VMEM((TAIL, wg), F32),
            pltpu.VMEM((TAIL, wg), F32),
        ],
        compiler_params=_cparams(2),
        name="ssd_scan",
    )(proj, proj, proj, proj, small, conv_w, conv_w, conv_w, conv_b, conv_b, conv_b,
      alog_row, dtb_row, d_row, norm_w, st0, tail_x, tail_bc, tail_bc)


def _out_proj_kernel(x_ref, o_ref, y_ref, wo_ref, wy_ref, h_ref):
    h_ref[...] = x_ref[...] + _dot(o_ref[...], wo_ref[...]) + _dot(y_ref[...], wy_ref[...])


def out_proj(x, o, y, w_out, *, tm, tn):
    rows, d = x.shape
    km = o.shape[1]
    return pl.pallas_call(
        _out_proj_kernel,
        grid=(d // tn, rows // tm),
        in_specs=[
            pl.BlockSpec((tm, tn), lambda j, i: (i, j)),
            pl.BlockSpec((tm, km), lambda j, i: (i, 0)),
            pl.BlockSpec((tm, km), lambda j, i: (i, 0)),
            pl.BlockSpec((km, tn), lambda j, i: (0, j)),
            pl.BlockSpec((km, tn), lambda j, i: (1, j)),
        ],
        out_specs=pl.BlockSpec((tm, tn), lambda j, i: (i, j)),
        out_shape=jax.ShapeDtypeStruct((rows, d), F32),
        compiler_params=_cparams(2),
        name="out_proj",
    )(x, o, y, w_out, w_out)


def _conv_ffn_kernel(h_ref, gn_ref, wg_ref, wv_ref, cwg_ref, cwv_ref, wd_ref, tg_ref, tv_ref,
                     fn_ref, o_ref, hn_sc, acc_sc, ug_sc, uv_sc, cg_sc, cv_sc, *,
                     tiles_per_batch, n_split):
    i = pl.program_id(0)
    f = pl.program_id(1)
    nf = pl.num_programs(1)
    tm = h_ref.shape[0]
    hm = tm // n_split

    @pl.when(f == 0)
    def _():
        h = h_ref[...]
        ms = jnp.mean(h * h, axis=-1, keepdims=True)
        hn_sc[...] = (h * lax.rsqrt(ms + NORM_EPS) * gn_ref[...]).astype(BF16)
        acc_sc[...] = jnp.zeros_like(acc_sc)

    first = (i % tiles_per_batch) == 0
    ug_sc[0:TAIL] = jnp.where(first, tg_ref[...], cg_sc[f])
    uv_sc[0:TAIL] = jnp.where(first, tv_ref[...], cv_sc[f])

    for s in range(n_split):
        rows = slice(s * hm, (s + 1) * hm)
        urows = slice(TAIL + s * hm, TAIL + (s + 1) * hm)
        hn = hn_sc[rows]
        ug_sc[urows] = _dot(hn, wg_ref[...])
        uv_sc[urows] = _dot(hn, wv_ref[...])
    cg_sc[f] = ug_sc[tm:tm + TAIL]
    cv_sc[f] = uv_sc[tm:tm + TAIL]
    for s in range(n_split):
        rows = slice(s * hm, (s + 1) * hm)
        gate = _causal_conv(ug_sc, TAIL + s * hm, hm, cwg_ref[...], 3)
        val = _causal_conv(uv_sc, TAIL + s * hm, hm, cwv_ref[...], 3)
        act = (_silu(gate) * val).astype(BF16)
        acc_sc[rows] += _dot(act, wd_ref[...])

    @pl.when(f == nf - 1)
    def _():
        h2 = h_ref[...] + acc_sc[...]
        ms = jnp.mean(h2 * h2, axis=-1, keepdims=True)
        o_ref[...] = h2 * lax.rsqrt(ms + NORM_EPS) * fn_ref[...]


def conv_ffn(h, gain, w_up, conv_w, w_down, u_tail, final_gain, *, tm, tf, tiles_per_batch,
             n_split):
    rows, d = h.shape
    dff = w_down.shape[0]
    nf = dff // tf
    kern = functools.partial(_conv_ffn_kernel, tiles_per_batch=tiles_per_batch, n_split=n_split)
    return pl.pallas_call(
        kern,
        grid=(rows // tm, nf),
        in_specs=[
            pl.BlockSpec((tm, d), lambda i, f: (i, 0)),
            pl.BlockSpec((1, d), lambda i, f: (0, 0)),
            pl.BlockSpec((d, tf), lambda i, f: (0, f)),
            pl.BlockSpec((d, tf), lambda i, f: (0, nf + f)),
            pl.BlockSpec((3, tf), lambda i, f: (0, f)),
            pl.BlockSpec((3, tf), lambda i, f: (0, nf + f)),
            pl.BlockSpec((tf, d), lambda i, f: (f, 0)),
            pl.BlockSpec((TAIL, tf), lambda i, f: (0, f)),
            pl.BlockSpec((TAIL, tf), lambda i, f: (0, nf + f)),
            pl.BlockSpec((1, d), lambda i, f: (0, 0)),
        ],
        out_specs=pl.BlockSpec((tm, d), lambda i, f: (i, 0)),
        out_shape=jax.ShapeDtypeStruct((rows, d), F32),
        scratch_shapes=[
            pltpu.VMEM((tm, d), BF16),
            pltpu.VMEM((tm, d), F32),
            pltpu.VMEM((TAIL + tm, tf), F32),
            pltpu.VMEM((TAIL + tm, tf), F32),
            pltpu.VMEM((nf, TAIL, tf), F32),
            pltpu.VMEM((nf, TAIL, tf), F32),
        ],
        compiler_params=_cparams(2),
        name="conv_ffn",
    )(h, gain, w_up, w_up, conv_w, conv_w, w_down, u_tail, u_tail, final_gain)


def _row_tile(rows, pref):
    t = min(rows, pref)
    while rows % t:
        t //= 2
    return t


def _lane_row(vec, offset):
    return jnp.zeros((1, LANES), F32).at[0, offset:offset + vec.shape[0]].set(vec.astype(F32))


def kernel(x, meta_tokens, norm_mix_w, w_in, dn_conv_w, dn_a_log, dn_dt_bias, dn_norm_w, m2_conv_w, m2_conv_b, m2_a_log, m2_dt_bias, m2_d, m2_norm_w, w_out, norm_ffn_w, ffn_up, ffn_conv_w, ffn_down, norm_final_w):
    batch, seq, d = x.shape
    depth = w_in.shape[0]
    assert depth == 1 and d == D_MODEL and seq % CHUNK == 0
    rows = batch * seq
    pad = CHUNK - N_META

    w = w_in[0]
    o_qkv, o_z, o_b, o_a = 0, 6144, 8192, 8208
    o_m2z, o_xbc, o_dt = 8224, 10272, 13344
    w_big = jnp.concatenate(
        [w[:, o_qkv:o_z], w[:, o_z:o_b], w[:, o_m2z:o_xbc], w[:, o_xbc:o_dt]], axis=1).astype(BF16)
    w_small = jnp.concatenate(
        [w[:, o_b:o_a], w[:, o_a:o_m2z], w[:, o_dt:], jnp.zeros((d, LANES - 64), w.dtype)],
        axis=1).astype(BF16)
    w_out16 = w_out[0].astype(BF16)
    ffn_up16 = ffn_up[0].astype(BF16)
    ffn_down16 = ffn_down[0].astype(BF16)
    zero_small = jnp.zeros((d, LANES), BF16)

    gain_mix = norm_mix_w[0].reshape(1, d).astype(F32)
    gain_ffn = norm_ffn_w[0].reshape(1, d).astype(F32)
    gain_fin = norm_final_w.reshape(1, d).astype(F32)
    dn_alog_row = _lane_row(dn_a_log[0], SM_A)
    dn_dtb_row = _lane_row(dn_dt_bias[0], SM_A)
    m2_alog_row = _lane_row(m2_a_log[0], SM_DT)
    m2_dtb_row = _lane_row(m2_dt_bias[0], SM_DT)
    dn_cw = dn_conv_w[0].astype(F32)
    dn_nw = dn_norm_w[0].reshape(1, DN_DK).astype(F32)
    m2_cw = m2_conv_w[0].astype(F32)
    m2_cb = m2_conv_b[0].reshape(1, -1).astype(F32)
    m2_d_row = jnp.repeat(m2_d[0].astype(F32), M2_P).reshape(1, M2_HEADS * M2_P)
    m2_nw = m2_norm_w[0].reshape(1, -1).astype(F32)
    ffn_cw = ffn_conv_w[0].astype(F32)

    hm = jnp.concatenate([jnp.zeros((pad, d), F32), meta_tokens.astype(F32)], axis=0)
    pm_big, pm_small = norm_matmul(hm, gain_mix, w_big, w_small, tm=CHUNK, tn=1024, out_dtype=BF16)
    zeros_tail = jnp.zeros((TAIL16, N_BIG), BF16)
    o_m, s_meta = gdn_scan(pm_big, pm_small, dn_cw, dn_alog_row, dn_dtb_row, dn_nw,
                           jnp.zeros((DN_HEADS, DN_DK, DN_DK), F32), zeros_tail[:, :3 * 2048],
                           batch=1, cps=1)
    y_m, st_meta = ssd_scan(pm_big, pm_small, m2_cw, m2_cb, m2_alog_row, m2_dtb_row, m2_d_row, m2_nw,
                            jnp.zeros((M2_GROUPS, M2_STATE, M2_GW), F32),
                            zeros_tail[:, :2048], jnp.zeros((TAIL, N_BIG - COL_B), F32),
                            batch=1, cps=1, n_pad=pad)
    h1_m = out_proj(hm, o_m, y_m, w_out16, tm=CHUNK, tn=1024)
    u_m, _ = norm_matmul(h1_m, gain_ffn, ffn_up16, zero_small, tm=CHUNK, tn=1024, out_dtype=F32)
    tail_big = pm_big[CHUNK - TAIL16:]
    tail_qkv = tail_big[:, COL_Q:COL_DNZ]
    tail_xs = tail_big[:, COL_XS:COL_B]
    tail_bc = tail_big[TAIL16 - TAIL:, COL_B:N_BIG].astype(F32)
    tail_u = u_m[CHUNK - TAIL:]

    x2 = x.reshape(rows, d).astype(F32)
    p_big, p_small = norm_matmul(x2, gain_mix, w_big, w_small, tm=_row_tile(rows, 1024), tn=1024,
                                 out_dtype=BF16)
    o_g, _ = gdn_scan(p_big, p_small, dn_cw, dn_alog_row, dn_dtb_row, dn_nw, s_meta[0], tail_qkv,
                      batch=batch, cps=GDN_CPS)
    y_g, _ = ssd_scan(p_big, p_small, m2_cw, m2_cb, m2_alog_row, m2_dtb_row, m2_d_row, m2_nw,
                      st_meta[0], tail_xs, tail_bc, batch=batch, cps=SSD_CPS, n_pad=0)
    h1 = out_proj(x2, o_g, y_g, w_out16, tm=_row_tile(rows, 512), tn=1024)
    tm_ffn = _row_tile(seq, 512)
    out = conv_ffn(h1, gain_ffn, ffn_up16, ffn_cw, ffn_down16, tail_u, gain_fin,
                   tm=tm_ffn, tf=512, tiles_per_batch=seq // tm_ffn, n_split=2)
    return out.reshape(batch, seq, d)
```

```python
import functools

import jax
import jax.numpy as jnp
from jax import lax
from jax.experimental import pallas as pl
from jax.experimental.pallas import tpu as pltpu

F32 = jnp.float32
BF16 = jnp.bfloat16

NORM_EPS = 1e-6
N_META = 16
CHUNK = 64
LANES = 128
TAIL = 8
TAIL16 = 16

DN_HEADS = 16
DN_DK = 128
M2_HEADS = 32
M2_P = 64
M2_GROUPS = 4
M2_STATE = 128
M2_GW = M2_HEADS * M2_P // M2_GROUPS

D_MODEL = 2048
COL_Q, COL_K, COL_V = 0, 2048, 4096
COL_DNZ = 6144
COL_M2Z = 8192
COL_XS = 10240
COL_B = 12288
COL_C = 12800
N_BIG = 13312
SM_BETA, SM_A, SM_DT = 0, 16, 32

VMEM_LIMIT = 56 * 1024 * 1024

GDN_CPS = 2
SSD_CPS = 2

def _cparams(n_axes):
    return pltpu.CompilerParams(dimension_semantics=("arbitrary",) * n_axes,
                                vmem_limit_bytes=VMEM_LIMIT)


def _dot(a, b):
    return jnp.dot(a, b, preferred_element_type=F32)


def _dot_nt(a, b):
    return lax.dot_general(a, b, (((1,), (1,)), ((), ())), preferred_element_type=F32)


def _bdot(a, b):
    return lax.dot_general(a, b, (((2,), (1,)), ((0,), (0,))), preferred_element_type=F32)


def _bdot_nt(a, b):
    return lax.dot_general(a, b, (((2,), (2,)), ((0,), (0,))), preferred_element_type=F32)


def _sigmoid(x):
    return 1.0 / (1.0 + jnp.exp(-x))


def _silu(x):
    return x * _sigmoid(x)


def _softplus(x):
    return jnp.maximum(x, 0.0) + jnp.log1p(jnp.exp(-jnp.abs(x)))


def _cumsum_rows(x):
    n = x.shape[0]
    row = lax.broadcasted_iota(jnp.int32, x.shape, 0)
    s = 1
    while s < n:
        x = x + jnp.where(row >= s, pltpu.roll(x, s, axis=0), 0.0)
        s *= 2
    return x


def _causal_conv(xe_ref, start, n, w, k):
    acc = None
    for j in range(k):
        term = xe_ref[pl.ds(start - (k - 1) + j, n), :] * w[j:j + 1]
        acc = term if acc is None else acc + term
    return acc


def _causal_conv_val(x, tail, w, k):
    xe = jnp.concatenate([tail, x], axis=0)
    acc = x * w[k - 1:k]
    for j in range(k - 1):
        acc = acc + pltpu.roll(xe, k - 1 - j, axis=0)[TAIL:] * w[j:j + 1]
    return acc


def _shift_matrix(k):
    m = TAIL16 + CHUNK
    r = lax.broadcasted_iota(jnp.int32, (CHUNK, (k - 1) * m), 0)
    c = lax.broadcasted_iota(jnp.int32, (CHUNK, (k - 1) * m), 1)
    hit = None
    for j in range(k - 1):
        sel = c == r + (j * m + TAIL16 - (k - 1) + j)
        hit = sel if hit is None else hit | sel
    return hit.astype(BF16)


def _causal_conv_bf16(x16, tail16, w, shift):
    n = x16.shape[0]
    k = w.shape[0]
    w16 = w.astype(BF16)
    outs = []
    for s in range(n // CHUNK):
        hist = tail16 if s == 0 else x16[s * CHUNK - TAIL16:s * CHUNK]
        xc = x16[s * CHUNK:(s + 1) * CHUNK]
        xe = jnp.concatenate([hist, xc], axis=0)
        taps = jnp.concatenate([xe * w16[j:j + 1] for j in range(k - 1)], axis=0)
        outs.append(_dot(shift, taps) + xc.astype(F32) * w[k - 1:k])
    return outs[0] if len(outs) == 1 else jnp.concatenate(outs, axis=0)


def _norm_matmul_kernel(x_ref, g_ref, w_ref, ws_ref, o_ref, os_ref, xn_ref):
    @pl.when(pl.program_id(1) == 0)
    def _():
        x = x_ref[...]
        ms = jnp.mean(x * x, axis=-1, keepdims=True)
        xn = (x * lax.rsqrt(ms + NORM_EPS) * g_ref[...]).astype(BF16)
        xn_ref[...] = xn
        os_ref[...] = _dot(xn, ws_ref[...])

    o_ref[...] = _dot(xn_ref[...], w_ref[...]).astype(o_ref.dtype)


def norm_matmul(x, gain, w, w_small, *, tm, tn, out_dtype):
    rows, k = x.shape
    n = w.shape[1]
    ns = w_small.shape[1]
    return pl.pallas_call(
        _norm_matmul_kernel,
        grid=(rows // tm, n // tn),
        in_specs=[
            pl.BlockSpec((tm, k), lambda i, j: (i, 0)),
            pl.BlockSpec((1, k), lambda i, j: (0, 0)),
            pl.BlockSpec((k, tn), lambda i, j: (0, j)),
            pl.BlockSpec((k, ns), lambda i, j: (0, 0)),
        ],
        out_specs=[
            pl.BlockSpec((tm, tn), lambda i, j: (i, j)),
            pl.BlockSpec((tm, ns), lambda i, j: (i, 0)),
        ],
        out_shape=[
            jax.ShapeDtypeStruct((rows, n), out_dtype),
            jax.ShapeDtypeStruct((rows, ns), F32),
        ],
        scratch_shapes=[pltpu.VMEM((tm, k), BF16)],
        compiler_params=_cparams(2),
        name="norm_matmul",
    )(x, gain, w, w_small)


def _gdn_kernel(q_ref, k_ref, v_ref, z_ref, sm_ref, wq_ref, wk_ref, wv_ref, alog_ref, dtb_ref,
                nw_ref, s0_ref, tq_ref, tk_ref, tv_ref,
                o_ref, sfin_ref, s_sc, tail_sc, *, cps):
    c = pl.program_id(1)
    rb = cps * CHUNK

    @pl.when(c == 0)
    def _():
        s_sc[...] = s0_ref[...]
        tail_sc[0] = tq_ref[...]
        tail_sc[1] = tk_ref[...]
        tail_sc[2] = tv_ref[...]

    shift = _shift_matrix(4)

    def conv_silu(x_ref, w_ref, idx):
        x16 = x_ref[...]
        y = _causal_conv_bf16(x16, tail_sc[idx], w_ref[...], shift)
        tail_sc[idx] = x16[rb - TAIL16:rb]
        return _silu(y)

    qs = conv_silu(q_ref, wq_ref, 0)
    ks = conv_silu(k_ref, wk_ref, 1)
    vs = conv_silu(v_ref, wv_ref, 2)

    sm = sm_ref[...]
    beta_all = _sigmoid(sm)
    g_all = -jnp.exp(alog_ref[...]) * _softplus(sm + dtb_ref[...])
    nw = nw_ref[...]

    ri = lax.broadcasted_iota(jnp.int32, (CHUNK, CHUNK), 0)
    ci = lax.broadcasted_iota(jnp.int32, (CHUNK, CHUNK), 1)
    causal = ri >= ci
    strict = ri > ci
    eye = (ri == ci).astype(F32)

    hs = range(DN_HEADS)

    def rows(cc):
        return slice(cc * CHUNK, (cc + 1) * CHUNK)

    def prepare(cc):
        def heads(a):
            return jnp.stack([a[rows(cc), h * DN_DK:(h + 1) * DN_DK] for h in hs], axis=0)

        def cols(a, base):
            return jnp.stack([a[:, base + h:base + h + 1] for h in hs], axis=0)

        gc = _cumsum_rows(g_all[rows(cc)])
        gct = jnp.concatenate([gc, gc], axis=0).T
        beta_c = beta_all[rows(cc)]
        eg_c = jnp.exp(gc)
        kdec_c = jnp.exp(gc[CHUNK - 1:CHUNK] - gc)
        beg_c = beta_c * pltpu.roll(eg_c, LANES - SM_A, axis=1)
        q = heads(qs)
        k = heads(ks)
        v = heads(vs)
        grow = jnp.stack([gct[SM_A + h:SM_A + h + 1, :CHUNK] for h in hs], axis=0)
        q = q * (lax.rsqrt(jnp.sum(q * q, axis=-1, keepdims=True) + NORM_EPS) * DN_DK ** -0.5)
        k = k * lax.rsqrt(jnp.sum(k * k, axis=-1, keepdims=True) + NORM_EPS)
        kd = k * cols(kdec_c, SM_A)
        return dict(
            beta=cols(beta_c, SM_BETA),
            decay=jnp.exp(jnp.where(causal[None], cols(gc, SM_A) - grow, -jnp.inf)),
            slast=cols(eg_c[CHUNK - 1:CHUNK], SM_A),
            qk16=jnp.concatenate([q.astype(BF16), k.astype(BF16)], axis=1),
            rhs=jnp.concatenate([v * cols(beta_c, SM_BETA), k * cols(beg_c, SM_BETA)],
                                axis=2).astype(BF16),
            qe16=(q * cols(eg_c, SM_A)).astype(BF16),
            kdt16=jnp.stack([kd[h].T for h in hs], axis=0).astype(BF16),
        )

    def chain(pr):
        gram = _bdot_nt(pr["qk16"], pr["qk16"][:, CHUNK:])
        qkm16 = (gram[:, :CHUNK] * pr["decay"]).astype(BF16)
        a_mat = jnp.where(strict[None], pr["beta"] * gram[:, CHUNK:] * pr["decay"], 0.0)
        x_inv = eye[None] - a_mat
        a16 = a_mat.astype(BF16)
        p = _bdot(a16, a16)
        for lvl in range(5):
            p16 = p.astype(BF16)
            if lvl < 4:
                xp = _bdot(jnp.concatenate([x_inv.astype(BF16), p16], axis=1), p16)
                x_inv = x_inv + xp[:, :CHUNK]
                p = xp[:, CHUNK:]
            else:
                x_inv = x_inv + _bdot(x_inv.astype(BF16), p16)
        wu = _bdot(x_inv.astype(BF16), pr["rhs"])
        return qkm16, wu[:, :, :DN_DK], wu[:, :, DN_DK:].astype(BF16)

    s = s_sc[...]
    for cc in range(cps):
        pr = prepare(cc)
        qkm16, u, w16 = chain(pr)
        s16 = s.astype(BF16)
        vn16 = (u - _bdot(w16, s16)).astype(BF16)
        o = _bdot(pr["qe16"], s16) + _bdot(qkm16, vn16)
        s = s * pr["slast"] + _bdot(pr["kdt16"], vn16)
        z = jnp.stack([z_ref[rows(cc), h * DN_DK:(h + 1) * DN_DK] for h in hs], axis=0).astype(F32)
        on = o * lax.rsqrt(jnp.mean(o * o, axis=-1, keepdims=True) + NORM_EPS) * nw * _silu(z)
        on = on.astype(o_ref.dtype)
        for h in hs:
            o_ref[rows(cc), h * DN_DK:(h + 1) * DN_DK] = on[h]
    s_sc[...] = s

    @pl.when(c == pl.num_programs(1) - 1)
    def _():
        sfin_ref[0] = s_sc[...]


def gdn_scan(proj, small, conv_w, alog_row, dtb_row, norm_w, s0, tail, *, batch, cps):
    rows = proj.shape[0]
    rb = cps * CHUNK
    ncb = rows // batch // rb
    w = DN_HEADS * DN_DK

    def col(off):
        return lambda b, c: (b * ncb + c, off // w)

    kern = functools.partial(_gdn_kernel, cps=cps)
    return pl.pallas_call(
        kern,
        grid=(batch, ncb),
        in_specs=[
            pl.BlockSpec((rb, w), col(COL_Q)),
            pl.BlockSpec((rb, w), col(COL_K)),
            pl.BlockSpec((rb, w), col(COL_V)),
            pl.BlockSpec((rb, w), col(COL_DNZ)),
            pl.BlockSpec((rb, LANES), lambda b, c: (b * ncb + c, 0)),
            pl.BlockSpec((4, w), lambda b, c: (0, 0)),
            pl.BlockSpec((4, w), lambda b, c: (0, 1)),
            pl.BlockSpec((4, w), lambda b, c: (0, 2)),
            pl.BlockSpec((1, LANES), lambda b, c: (0, 0)),
            pl.BlockSpec((1, LANES), lambda b, c: (0, 0)),
            pl.BlockSpec((1, DN_DK), lambda b, c: (0, 0)),
            pl.BlockSpec((DN_HEADS, DN_DK, DN_DK), lambda b, c: (0, 0, 0)),
            pl.BlockSpec((TAIL16, w), lambda b, c: (0, 0)),
            pl.BlockSpec((TAIL16, w), lambda b, c: (0, 1)),
            pl.BlockSpec((TAIL16, w), lambda b, c: (0, 2)),
        ],
        out_specs=[
            pl.BlockSpec((rb, w), lambda b, c: (b * ncb + c, 0)),
            pl.BlockSpec((1, DN_HEADS, DN_DK, DN_DK), lambda b, c: (b, 0, 0, 0)),
        ],
        out_shape=[
            jax.ShapeDtypeStruct((rows, w), BF16),
            jax.ShapeDtypeStruct((batch, DN_HEADS, DN_DK, DN_DK), F32),
        ],
        scratch_shapes=[
            pltpu.VMEM((DN_HEADS, DN_DK, DN_DK), F32),
            pltpu.VMEM((3, TAIL16, w), BF16),
        ],
        compiler_params=_cparams(2),
        name="gdn_scan",
    )(proj, proj, proj, proj, small, conv_w, conv_w, conv_w, alog_row, dtb_row, norm_w, s0,
      tail, tail, tail)


def _ssd_kernel(x_ref, b_ref, c_ref, z_ref, sm_ref, wx_ref, wb_ref, wc_ref, bx_ref, bb_ref, bc_ref,
                alog_ref, dtb_ref, d_ref, nw_ref, st0_ref, tx_ref, tb_ref, tc_ref,
                y_ref, stfin_ref, st_sc, tx_sc, tb_sc, tc_sc, *, cps, n_pad):
    c = pl.program_id(1)
    rb = cps * CHUNK

    @pl.when(c == 0)
    def _():
        st_sc[...] = st0_ref[...]
        tx_sc[...] = tx_ref[...]
        tb_sc[...] = tb_ref[...]
        tc_sc[...] = tc_ref[...]

    shift = _shift_matrix(4)

    def conv_silu(x_ref, w_ref, bias_ref, t_sc):
        x16 = x_ref[...]
        y = _causal_conv_bf16(x16, t_sc[...], w_ref[...], shift) + bias_ref[...]
        t_sc[...] = x16[rb - TAIL16:rb]
        return _silu(y)

    def conv_silu_f32(x_ref, w_ref, bias_ref, t_sc):
        x = x_ref[...].astype(F32)
        y = _causal_conv_val(x, t_sc[...], w_ref[...], 4) + bias_ref[...]
        t_sc[...] = x[rb - TAIL:rb]
        return _silu(y)

    xs = conv_silu(x_ref, wx_ref, bx_ref, tx_sc)
    bm = conv_silu_f32(b_ref, wb_ref, bb_ref, tb_sc)
    cm = conv_silu_f32(c_ref, wc_ref, bc_ref, tc_sc)

    sm = sm_ref[...]
    dt_all = _softplus(sm + dtb_ref[...])
    a_all = dt_all * (-jnp.exp(alog_ref[...]))
    d_row = d_ref[...]
    nw = nw_ref[...]

    lane = lax.broadcasted_iota(jnp.int32, (CHUNK, LANES), 1)
    rowi = lax.broadcasted_iota(jnp.int32, (CHUNK, LANES), 0)
    lo = lane < M2_P
    causal2 = rowi >= jnp.where(lo, lane, lane - M2_P)
    lo2 = lax.broadcasted_iota(jnp.int32, (2 * CHUNK, LANES), 1) < M2_P
    top2 = lax.broadcasted_iota(jnp.int32, (2 * CHUNK, LANES), 0) < CHUNK
    blk2 = lo2 == top2

    npair = M2_HEADS // 2
    ppg = npair // M2_GROUPS

    for cc in range(cps):
        r = slice(cc * CHUNK, (cc + 1) * CHUNK)
        acs = _cumsum_rows(a_all[r])
        zt = jnp.concatenate([acs, pltpu.roll(acs, LANES - 1, axis=1)], axis=0).T
        dt = dt_all[r]
        y_tiles = []
        for g in range(M2_GROUPS):
            gs = slice(g * M2_STATE, (g + 1) * M2_STATE)
            b16 = bm[r, gs].astype(BF16)
            c16 = cm[r, gs].astype(BF16)
            cbcb = _dot_nt(c16, jnp.concatenate([b16, b16], axis=0))
            st = st_sc[g]
            yoff = _dot(c16, st.astype(BF16))
            xdec_tiles = []
            ea_last = []
            for mm in range(ppg):
                m = g * ppg + mm
                la = SM_DT + 2 * m
                ea = jnp.where(lo, acs[:, la:la + 1], acs[:, la + 1:la + 2])
                edt = jnp.where(lo, dt[:, la:la + 1], dt[:, la + 1:la + 2])
                xs_m = xs[r, m * LANES:(m + 1) * LANES]
                xdt = xs_m * edt
                if n_pad:
                    xdt = jnp.where(rowi + cc * CHUNK >= n_pad, xdt, 0.0)
                rowp = zt[la:la + 1, :]
                lmat = jnp.exp(jnp.where(causal2, ea - rowp, -jnp.inf))
                wp = (cbcb * lmat).astype(BF16)
                x16 = xdt.astype(BF16)
                x2 = jnp.concatenate([x16, x16], axis=0)
                rhs = jnp.where(blk2, x2, jnp.zeros_like(x2))
                y = _dot(wp, rhs) + yoff[:, mm * LANES:(mm + 1) * LANES] * jnp.exp(ea)
                y_tiles.append(y + d_row[:, m * LANES:(m + 1) * LANES] * xs_m)
                eal = ea[CHUNK - 1:CHUNK]
                ea_last.append(eal)
                xdec_tiles.append((xdt * jnp.exp(eal - ea)).astype(BF16))
            xdec = jnp.concatenate(xdec_tiles, axis=1)
            sdec = jnp.exp(jnp.concatenate(ea_last, axis=1))
            bt16 = bm[r, gs].T.astype(BF16)
            st_sc[g] = st * sdec + _dot(bt16, xdec)
        for g in range(M2_GROUPS):
            tiles = []
            ssq = None
            for mm in range(ppg):
                m = g * ppg + mm
                z = z_ref[r, m * LANES:(m + 1) * LANES].astype(F32)
                t = y_tiles[m] * _silu(z)
                tiles.append(t)
                part = jnp.sum(t * t, axis=-1, keepdims=True)
                ssq = part if ssq is None else ssq + part
            scale = lax.rsqrt(ssq * (1.0 / M2_GW) + NORM_EPS)
            for mm in range(ppg):
                m = g * ppg + mm
                cs = slice(m * LANES, (m + 1) * LANES)
                y_ref[r, cs] = (tiles[mm] * scale * nw[:, cs]).astype(y_ref.dtype)

    @pl.when(c == pl.num_programs(1) - 1)
    def _():
        stfin_ref[0] = st_sc[...]


def ssd_scan(proj, small, conv_w, conv_b, alog_row, dtb_row, d_row, norm_w, st0, tail_x, tail_bc, *,
             batch, cps, n_pad):
    rows = proj.shape[0]
    rb = cps * CHUNK
    ncb = rows // batch // rb
    wx = M2_HEADS * M2_P
    wg = M2_GROUPS * M2_STATE

    def rowblk(width, off):
        return pl.BlockSpec((rb, width), lambda b, c: (b * ncb + c, off // width))

    def const(shape, *idx):
        return pl.BlockSpec(shape, lambda b, c: idx)

    kern = functools.partial(_ssd_kernel, cps=cps, n_pad=n_pad)
    return pl.pallas_call(
        kern,
        grid=(batch, ncb),
        in_specs=[
            rowblk(wx, COL_XS),
            rowblk(wg, COL_B),
            rowblk(wg, COL_C),
            rowblk(wx, COL_M2Z),
            pl.BlockSpec((rb, LANES), lambda b, c: (b * ncb + c, 0)),
            const((4, wx), 0, 0),
            const((4, wg), 0, wx // wg),
            const((4, wg), 0, wx // wg + 1),
            const((1, wx), 0, 0),
            const((1, wg), 0, wx // wg),
            const((1, wg), 0, wx // wg + 1),
            const((1, LANES), 0, 0),
            const((1, LANES), 0, 0),
            const((1, wx), 0, 0),
            const((1, wx), 0, 0),
            const((M2_GROUPS, M2_STATE, M2_GW), 0, 0, 0),
            const((TAIL16, wx), 0, 0),
            const((TAIL, wg), 0, 0),
            const((TAIL, wg), 0, 1),
        ],
        out_specs=[
            pl.BlockSpec((rb, wx), lambda b, c: (b * ncb + c, 0)),
            pl.BlockSpec((1, M2_GROUPS, M2_STATE, M2_GW), lambda b, c: (b, 0, 0, 0)),
        ],
        out_shape=[
            jax.ShapeDtypeStruct((rows, wx), BF16),
            jax.ShapeDtypeStruct((batch, M2_GROUPS, M2_STATE, M2_GW), F32),
        ],
        scratch_shapes=[
            pltpu.VMEM((M2_GROUPS, M2_STATE, M2_GW), F32),
            pltpu.VMEM((TAIL16, wx), BF16),
            pltpu.VMEM((TAIL, wg), F32),
            pltpu.VMEM((TAIL, wg), F32),
        ],
        compiler_params=_cparams(2),
        name="ssd_scan",
    )(proj, proj, proj, proj, small, conv_w, conv_w, conv_w, conv_b, conv_b, conv_b,
      alog_row, dtb_row, d_row, norm_w, st0, tail_x, tail_bc, tail_bc)


def _out_proj_kernel(x_ref, o_ref, y_ref, wo_ref, wy_ref, h_ref):
    h_ref[...] = x_ref[...] + _dot(o_ref[...], wo_ref[...]) + _dot(y_ref[...], wy_ref[...])


def out_proj(x, o, y, w_out, *, tm, tn):
    rows, d = x.shape
    km = o.shape[1]
    return pl.pallas_call(
        _out_proj_kernel,
        grid=(d // tn, rows // tm),
        in_specs=[
            pl.BlockSpec((tm, tn), lambda j, i: (i, j)),
            pl.BlockSpec((tm, km), lambda j, i: (i, 0)),
            pl.BlockSpec((tm, km), lambda j, i: (i, 0)),
            pl.BlockSpec((km, tn), lambda j, i: (0, j)),
            pl.BlockSpec((km, tn), lambda j, i: (1, j)),
        ],
        out_specs=pl.BlockSpec((tm, tn), lambda j, i: (i, j)),
        out_shape=jax.ShapeDtypeStruct((rows, d), F32),
        compiler_params=_cparams(2),
        name="out_proj",
    )(x, o, y, w_out, w_out)


def _conv_ffn_kernel(h_ref, gn_ref, wg_ref, wv_ref, cwg_ref, cwv_ref, wd_ref, tg_ref, tv_ref,
                     fn_ref, o_ref, hn_sc, acc_sc, ug_sc, uv_sc, cg_sc, cv_sc, *,
                     tiles_per_batch, n_split):
    i = pl.program_id(0)
    f = pl.program_id(1)
    nf = pl.num_programs(1)
    tm = h_ref.shape[0]
    hm = tm // n_split

    @pl.when(f == 0)
    def _():
        h = h_ref[...]
        ms = jnp.mean(h * h, axis=-1, keepdims=True)
        hn_sc[...] = (h * lax.rsqrt(ms + NORM_EPS) * gn_ref[...]).astype(BF16)
        acc_sc[...] = jnp.zeros_like(acc_sc)

    first = (i % tiles_per_batch) == 0
    ug_sc[0:TAIL] = jnp.where(first, tg_ref[...], cg_sc[f])
    uv_sc[0:TAIL] = jnp.where(first, tv_ref[...], cv_sc[f])

    for s in range(n_split):
        rows = slice(s * hm, (s + 1) * hm)
        urows = slice(TAIL + s * hm, TAIL + (s + 1) * hm)
        hn = hn_sc[rows]
        ug_sc[urows] = _dot(hn, wg_ref[...])
        uv_sc[urows] = _dot(hn, wv_ref[...])
    cg_sc[f] = ug_sc[tm:tm + TAIL]
    cv_sc[f] = uv_sc[tm:tm + TAIL]
    for s in range(n_split):
        rows = slice(s * hm, (s + 1) * hm)
        gate = _causal_conv(ug_sc, TAIL + s * hm, hm, cwg_ref[...], 3)
        val = _causal_conv(uv_sc, TAIL + s * hm, hm, cwv_ref[...], 3)
        act = (_silu(gate) * val).astype(BF16)
        acc_sc[rows] += _dot(act, wd_ref[...])

    @pl.when(f == nf - 1)
    def _():
        h2 = h_ref[...] + acc_sc[...]
        ms = jnp.mean(h2 * h2, axis=-1, keepdims=True)
        o_ref[...] = h2 * lax.rsqrt(ms + NORM_EPS) * fn_ref[...]


def conv_ffn(h, gain, w_up, conv_w, w_down, u_tail, final_gain, *, tm, tf, tiles_per_batch,
             n_split):
    rows, d = h.shape
    dff = w_down.shape[0]
    nf = dff // tf
    kern = functools.partial(_conv_ffn_kernel, tiles_per_batch=tiles_per_batch, n_split=n_split)
    return pl.pallas_call(
        kern,
        grid=(rows // tm, nf),
        in_specs=[
            pl.BlockSpec((tm, d), lambda i, f: (i, 0)),
            pl.BlockSpec((1, d), lambda i, f: (0, 0)),
            pl.BlockSpec((d, tf), lambda i, f: (0, f)),
            pl.BlockSpec((d, tf), lambda i, f: (0, nf + f)),
            pl.BlockSpec((3, tf), lambda i, f: (0, f)),
            pl.BlockSpec((3, tf), lambda i, f: (0, nf + f)),
            pl.BlockSpec((tf, d), lambda i, f: (f, 0)),
            pl.BlockSpec((TAIL, tf), lambda i, f: (0, f)),
            pl.BlockSpec((TAIL, tf), lambda i, f: (0, nf + f)),
            pl.BlockSpec((1, d), lambda i, f: (0, 0)),
        ],
        out_specs=pl.BlockSpec((tm, d), lambda i, f: (i, 0)),
        out_shape=jax.ShapeDtypeStruct((rows, d), F32),
        scratch_shapes=[
            pltpu.VMEM((tm, d), BF16),
            pltpu.VMEM((tm, d), F32),
            pltpu.VMEM((TAIL + tm, tf), F32),
            pltpu.VMEM((TAIL + tm, tf), F32),
            pltpu.VMEM((nf, TAIL, tf), F32),
            pltpu.VMEM((nf, TAIL, tf), F32),
        ],
        compiler_params=_cparams(2),
        name="conv_ffn",
    )(h, gain, w_up, w_up, conv_w, conv_w, w_down, u_tail, u_tail, final_gain)


def _row_tile(rows, pref):
    t = min(rows, pref)
    while rows % t:
        t //= 2
    return t


def _lane_row(vec, offset):
    return jnp.zeros((1, LANES), F32).at[0, offset:offset + vec.shape[0]].set(vec.astype(F32))


def kernel(x, meta_tokens, norm_mix_w, w_in, dn_conv_w, dn_a_log, dn_dt_bias, dn_norm_w, m2_conv_w, m2_conv_b, m2_a_log, m2_dt_bias, m2_d, m2_norm_w, w_out, norm_ffn_w, ffn_up, ffn_conv_w, ffn_down, norm_final_w):
    batch, seq, d = x.shape
    depth = w_in.shape[0]
    assert depth == 1 and d == D_MODEL and seq % CHUNK == 0
    rows = batch * seq
    pad = CHUNK - N_META

    w = w_in[0]
    o_qkv, o_z, o_b, o_a = 0, 6144, 8192, 8208
    o_m2z, o_xbc, o_dt = 8224, 10272, 13344
    w_big = jnp.concatenate(
        [w[:, o_qkv:o_z], w[:, o_z:o_b], w[:, o_m2z:o_xbc], w[:, o_xbc:o_dt]], axis=1).astype(BF16)
    w_small = jnp.concatenate(
        [w[:, o_b:o_a], w[:, o_a:o_m2z], w[:, o_dt:], jnp.zeros((d, LANES - 64), w.dtype)],
        axis=1).astype(BF16)
    w_out16 = w_out[0].astype(BF16)
    ffn_up16 = ffn_up[0].astype(BF16)
    ffn_down16 = ffn_down[0].astype(BF16)
    zero_small = jnp.zeros((d, LANES), BF16)

    gain_mix = norm_mix_w[0].reshape(1, d).astype(F32)
    gain_ffn = norm_ffn_w[0].reshape(1, d).astype(F32)
    gain_fin = norm_final_w.reshape(1, d).astype(F32)
    dn_alog_row = _lane_row(dn_a_log[0], SM_A)
    dn_dtb_row = _lane_row(dn_dt_bias[0], SM_A)
    m2_alog_row = _lane_row(m2_a_log[0], SM_DT)
    m2_dtb_row = _lane_row(m2_dt_bias[0], SM_DT)
    dn_cw = dn_conv_w[0].astype(F32)
    dn_nw = dn_norm_w[0].reshape(1, DN_DK).astype(F32)
    m2_cw = m2_conv_w[0].astype(F32)
    m2_cb = m2_conv_b[0].reshape(1, -1).astype(F32)
    m2_d_row = jnp.repeat(m2_d[0].astype(F32), M2_P).reshape(1, M2_HEADS * M2_P)
    m2_nw = m2_norm_w[0].reshape(1, -1).astype(F32)
    ffn_cw = ffn_conv_w[0].astype(F32)

    hm = jnp.concatenate([jnp.zeros((pad, d), F32), meta_tokens.astype(F32)], axis=0)
    pm_big, pm_small = norm_matmul(hm, gain_mix, w_big, w_small, tm=CHUNK, tn=1024, out_dtype=BF16)
    zeros_tail = jnp.zeros((TAIL16, N_BIG), BF16)
    o_m, s_meta = gdn_scan(pm_big, pm_small, dn_cw, dn_alog_row, dn_dtb_row, dn_nw,
                           jnp.zeros((DN_HEADS, DN_DK, DN_DK), F32), zeros_tail[:, :3 * 2048],
                           batch=1, cps=1)
    y_m, st_meta = ssd_scan(pm_big, pm_small, m2_cw, m2_cb, m2_alog_row, m2_dtb_row, m2_d_row, m2_nw,
                            jnp.zeros((M2_GROUPS, M2_STATE, M2_GW), F32),
                            zeros_tail[:, :2048], jnp.zeros((TAIL, N_BIG - COL_B), F32),
                            batch=1, cps=1, n_pad=pad)
    h1_m = out_proj(hm, o_m, y_m, w_out16, tm=CHUNK, tn=1024)
    u_m, _ = norm_matmul(h1_m, gain_ffn, ffn_up16, zero_small, tm=CHUNK, tn=1024, out_dtype=F32)
    tail_big = pm_big[CHUNK - TAIL16:]
    tail_qkv = tail_big[:, COL_Q:COL_DNZ]
    tail_xs = tail_big[:, COL_XS:COL_B]
    tail_bc = tail_big[TAIL16 - TAIL:, COL_B:N_BIG].astype(F32)
    tail_u = u_m[CHUNK - TAIL:]

    x2 = x.reshape(rows, d).astype(F32)
    p_big, p_small = norm_matmul(x2, gain_mix, w_big, w_small, tm=_row_tile(rows, 512), tn=3328,
                                 out_dtype=BF16)
    o_g, _ = gdn_scan(p_big, p_small, dn_cw, dn_alog_row, dn_dtb_row, dn_nw, s_meta[0], tail_qkv,
                      batch=batch, cps=GDN_CPS)
    y_g, _ = ssd_scan(p_big, p_small, m2_cw, m2_cb, m2_alog_row, m2_dtb_row, m2_d_row, m2_nw,
                      st_meta[0], tail_xs, tail_bc, batch=batch, cps=SSD_CPS, n_pad=0)
    h1 = out_proj(x2, o_g, y_g, w_out16, tm=_row_tile(rows, 512), tn=1024)
    tm_ffn = _row_tile(seq, 512)
    out = conv_ffn(h1, gain_ffn, ffn_up16, ffn_cw, ffn_down16, tail_u, gain_fin,
                   tm=tm_ffn, tf=512, tiles_per_batch=seq // tm_ffn, n_split=2)
    return out.reshape(batch, seq, d)
```

```python
import functools

import jax
import jax.numpy as jnp
from jax import lax
from jax.experimental import pallas as pl
from jax.experimental.pallas import tpu as pltpu

F32 = jnp.float32
BF16 = jnp.bfloat16

NORM_EPS = 1e-6
N_META = 16
CHUNK = 64
LANES = 128
TAIL = 8
TAIL16 = 16

DN_HEADS = 16
DN_DK = 128
M2_HEADS = 32
M2_P = 64
M2_GROUPS = 4
M2_STATE = 128
M2_GW = M2_HEADS * M2_P // M2_GROUPS

D_MODEL = 2048
COL_Q, COL_K, COL_V = 0, 2048, 4096
COL_DNZ = 6144
COL_M2Z = 8192
COL_XS = 10240
COL_B = 12288
COL_C = 12800
N_BIG = 13312
SM_BETA, SM_A, SM_DT = 0, 16, 32

VMEM_LIMIT = 56 * 1024 * 1024

GDN_CPS = 2
SSD_CPS = 2
GDN_GROUPS = 2

def _cparams(n_axes):
    return pltpu.CompilerParams(dimension_semantics=("arbitrary",) * n_axes,
                                vmem_limit_bytes=VMEM_LIMIT)


def _dot(a, b):
    return jnp.dot(a, b, preferred_element_type=F32)


def _dot_nt(a, b):
    return lax.dot_general(a, b, (((1,), (1,)), ((), ())), preferred_element_type=F32)


def _bdot(a, b):
    return lax.dot_general(a, b, (((2,), (1,)), ((0,), (0,))), preferred_element_type=F32)


def _bdot_nt(a, b):
    return lax.dot_general(a, b, (((2,), (2,)), ((0,), (0,))), preferred_element_type=F32)


def _sigmoid(x):
    return 1.0 / (1.0 + jnp.exp(-x))


def _silu(x):
    return x * _sigmoid(x)


def _softplus(x):
    return jnp.maximum(x, 0.0) + jnp.log1p(jnp.exp(-jnp.abs(x)))


def _cumsum_rows(x):
    n = x.shape[0]
    row = lax.broadcasted_iota(jnp.int32, x.shape, 0)
    s = 1
    while s < n:
        x = x + jnp.where(row >= s, pltpu.roll(x, s, axis=0), 0.0)
        s *= 2
    return x


def _causal_conv(xe_ref, start, n, w, k):
    acc = None
    for j in range(k):
        term = xe_ref[pl.ds(start - (k - 1) + j, n), :] * w[j:j + 1]
        acc = term if acc is None else acc + term
    return acc


def _causal_conv_val(x, tail, w, k):
    xe = jnp.concatenate([tail, x], axis=0)
    acc = x * w[k - 1:k]
    for j in range(k - 1):
        acc = acc + pltpu.roll(xe, k - 1 - j, axis=0)[TAIL:] * w[j:j + 1]
    return acc


def _shift_matrix(k):
    m = TAIL16 + CHUNK
    r = lax.broadcasted_iota(jnp.int32, (CHUNK, (k - 1) * m), 0)
    c = lax.broadcasted_iota(jnp.int32, (CHUNK, (k - 1) * m), 1)
    hit = None
    for j in range(k - 1):
        sel = c == r + (j * m + TAIL16 - (k - 1) + j)
        hit = sel if hit is None else hit | sel
    return hit.astype(BF16)


def _causal_conv_bf16(x16, tail16, w, shift):
    n = x16.shape[0]
    k = w.shape[0]
    w16 = w.astype(BF16)
    outs = []
    for s in range(n // CHUNK):
        hist = tail16 if s == 0 else x16[s * CHUNK - TAIL16:s * CHUNK]
        xc = x16[s * CHUNK:(s + 1) * CHUNK]
        xe = jnp.concatenate([hist, xc], axis=0)
        taps = jnp.concatenate([xe * w16[j:j + 1] for j in range(k - 1)], axis=0)
        outs.append(_dot(shift, taps) + xc.astype(F32) * w[k - 1:k])
    return outs[0] if len(outs) == 1 else jnp.concatenate(outs, axis=0)


def _norm_matmul_kernel(x_ref, g_ref, w_ref, ws_ref, o_ref, os_ref, xn_ref):
    @pl.when(pl.program_id(1) == 0)
    def _():
        x = x_ref[...]
        ms = jnp.mean(x * x, axis=-1, keepdims=True)
        xn = (x * lax.rsqrt(ms + NORM_EPS) * g_ref[...]).astype(BF16)
        xn_ref[...] = xn
        os_ref[...] = _dot(xn, ws_ref[...])

    o_ref[...] = _dot(xn_ref[...], w_ref[...]).astype(o_ref.dtype)


def norm_matmul(x, gain, w, w_small, *, tm, tn, out_dtype):
    rows, k = x.shape
    n = w.shape[1]
    ns = w_small.shape[1]
    return pl.pallas_call(
        _norm_matmul_kernel,
        grid=(rows // tm, n // tn),
        in_specs=[
            pl.BlockSpec((tm, k), lambda i, j: (i, 0)),
            pl.BlockSpec((1, k), lambda i, j: (0, 0)),
            pl.BlockSpec((k, tn), lambda i, j: (0, j)),
            pl.BlockSpec((k, ns), lambda i, j: (0, 0)),
        ],
        out_specs=[
            pl.BlockSpec((tm, tn), lambda i, j: (i, j)),
            pl.BlockSpec((tm, ns), lambda i, j: (i, 0)),
        ],
        out_shape=[
            jax.ShapeDtypeStruct((rows, n), out_dtype),
            jax.ShapeDtypeStruct((rows, ns), F32),
        ],
        scratch_shapes=[pltpu.VMEM((tm, k), BF16)],
        compiler_params=_cparams(2),
        name="norm_matmul",
    )(x, gain, w, w_small)


def _gdn_kernel(q_ref, k_ref, v_ref, z_ref, sm_ref, wq_ref, wk_ref, wv_ref, alog_ref, dtb_ref,
                nw_ref, s0_ref, tq_ref, tk_ref, tv_ref,
                o_ref, sfin_ref, s_sc, tail_sc, *, cps):
    c = pl.program_id(1)
    rb = cps * CHUNK

    @pl.when(c == 0)
    def _():
        s_sc[...] = s0_ref[...]
        tail_sc[0] = tq_ref[...]
        tail_sc[1] = tk_ref[...]
        tail_sc[2] = tv_ref[...]

    shift = _shift_matrix(4)

    def conv_silu(x_ref, w_ref, idx):
        x16 = x_ref[...]
        y = _causal_conv_bf16(x16, tail_sc[idx], w_ref[...], shift)
        tail_sc[idx] = x16[rb - TAIL16:rb]
        return _silu(y)

    qs = conv_silu(q_ref, wq_ref, 0)
    ks = conv_silu(k_ref, wk_ref, 1)
    vs = conv_silu(v_ref, wv_ref, 2)

    sm = sm_ref[...]
    beta_all = _sigmoid(sm)
    g_all = -jnp.exp(alog_ref[...]) * _softplus(sm + dtb_ref[...])
    nw = nw_ref[...]

    ri = lax.broadcasted_iota(jnp.int32, (CHUNK, CHUNK), 0)
    ci = lax.broadcasted_iota(jnp.int32, (CHUNK, CHUNK), 1)
    causal = ri >= ci
    strict = ri > ci
    eye = (ri == ci).astype(F32)

    def rows(cc):
        return slice(cc * CHUNK, (cc + 1) * CHUNK)

    def prepare(cc, hs):
        def heads(a):
            return jnp.stack([a[rows(cc), h * DN_DK:(h + 1) * DN_DK] for h in hs], axis=0)

        def cols(a, base):
            return jnp.stack([a[:, base + h:base + h + 1] for h in hs], axis=0)

        gc = _cumsum_rows(g_all[rows(cc)])
        gct = jnp.concatenate([gc, gc], axis=0).T
        beta_c = beta_all[rows(cc)]
        eg_c = jnp.exp(gc)
        kdec_c = jnp.exp(gc[CHUNK - 1:CHUNK] - gc)
        beg_c = beta_c * pltpu.roll(eg_c, LANES - SM_A, axis=1)
        q = heads(qs)
        k = heads(ks)
        v = heads(vs)
        grow = jnp.stack([gct[SM_A + h:SM_A + h + 1, :CHUNK] for h in hs], axis=0)
        q = q * (lax.rsqrt(jnp.sum(q * q, axis=-1, keepdims=True) + NORM_EPS) * DN_DK ** -0.5)
        k = k * lax.rsqrt(jnp.sum(k * k, axis=-1, keepdims=True) + NORM_EPS)
        kd = k * cols(kdec_c, SM_A)
        return dict(
            beta=cols(beta_c, SM_BETA),
            decay=jnp.exp(jnp.where(causal[None], cols(gc, SM_A) - grow, -jnp.inf)),
            slast=cols(eg_c[CHUNK - 1:CHUNK], SM_A),
            qk16=jnp.concatenate([q.astype(BF16), k.astype(BF16)], axis=1),
            rhs=jnp.concatenate([v * cols(beta_c, SM_BETA), k * cols(beg_c, SM_BETA)],
                                axis=2).astype(BF16),
            qe16=(q * cols(eg_c, SM_A)).astype(BF16),
            kdt16=jnp.stack([kd[i].T for i in range(len(hs))], axis=0).astype(BF16),
        )

    def chunk_steps(cc, hs, state):
        pr = prepare(cc, hs)
        gram = _bdot_nt(pr["qk16"], pr["qk16"][:, CHUNK:])
        yield
        qkm16 = (gram[:, :CHUNK] * pr["decay"]).astype(BF16)
        a_mat = jnp.where(strict[None], pr["beta"] * gram[:, CHUNK:] * pr["decay"], 0.0)
        x_inv = eye[None] - a_mat
        a16 = a_mat.astype(BF16)
        p = _bdot(a16, a16)
        yield
        for lvl in range(5):
            p16 = p.astype(BF16)
            if lvl < 4:
                xp = _bdot(jnp.concatenate([x_inv.astype(BF16), p16], axis=1), p16)
                yield
                x_inv = x_inv + xp[:, :CHUNK]
                p = xp[:, CHUNK:]
            else:
                last = _bdot(x_inv.astype(BF16), p16)
                yield
                x_inv = x_inv + last
        wu = _bdot(x_inv.astype(BF16), pr["rhs"])
        yield
        u = wu[:, :, :DN_DK]
        w16 = wu[:, :, DN_DK:].astype(BF16)
        assert state["done"] == cc
        s = state["s"]
        s16 = s.astype(BF16)
        ws = _bdot(w16, s16)
        yield
        vn16 = (u - ws).astype(BF16)
        o = _bdot(pr["qe16"], s16) + _bdot(qkm16, vn16)
        state["s"] = s * pr["slast"] + _bdot(pr["kdt16"], vn16)
        state["done"] = cc + 1
        yield
        z = jnp.stack([z_ref[rows(cc), h * DN_DK:(h + 1) * DN_DK] for h in hs], axis=0).astype(F32)
        on = o * lax.rsqrt(jnp.mean(o * o, axis=-1, keepdims=True) + NORM_EPS) * nw * _silu(z)
        on = on.astype(o_ref.dtype)
        for i, h in enumerate(hs):
            o_ref[rows(cc), h * DN_DK:(h + 1) * DN_DK] = on[i]

    gsz = DN_HEADS // GDN_GROUPS
    groups = [list(range(g * gsz, (g + 1) * gsz)) for g in range(GDN_GROUPS)]
    states = [dict(s=s_sc[hs[0]:hs[-1] + 1], done=0) for hs in groups]
    for cc in range(cps):
        active = [chunk_steps(cc, hs, st) for hs, st in zip(groups, states)]
        while active:
            for gen in list(active):
                try:
                    next(gen)
                except StopIteration:
                    active.remove(gen)
    for hs, st in zip(groups, states):
        s_sc[hs[0]:hs[-1] + 1] = st["s"]

    @pl.when(c == pl.num_programs(1) - 1)
    def _():
        sfin_ref[0] = s_sc[...]


def gdn_scan(proj, small, conv_w, alog_row, dtb_row, norm_w, s0, tail, *, batch, cps):
    rows = proj.shape[0]
    rb = cps * CHUNK
    ncb = rows // batch // rb
    w = DN_HEADS * DN_DK

    def col(off):
        return lambda b, c: (b * ncb + c, off // w)

    kern = functools.partial(_gdn_kernel, cps=cps)
    return pl.pallas_call(
        kern,
        grid=(batch, ncb),
        in_specs=[
            pl.BlockSpec((rb, w), col(COL_Q)),
            pl.BlockSpec((rb, w), col(COL_K)),
            pl.BlockSpec((rb, w), col(COL_V)),
            pl.BlockSpec((rb, w), col(COL_DNZ)),
            pl.BlockSpec((rb, LANES), lambda b, c: (b * ncb + c, 0)),
            pl.BlockSpec((4, w), lambda b, c: (0, 0)),
            pl.BlockSpec((4, w), lambda b, c: (0, 1)),
            pl.BlockSpec((4, w), lambda b, c: (0, 2)),
            pl.BlockSpec((1, LANES), lambda b, c: (0, 0)),
            pl.BlockSpec((1, LANES), lambda b, c: (0, 0)),
            pl.BlockSpec((1, DN_DK), lambda b, c: (0, 0)),
            pl.BlockSpec((DN_HEADS, DN_DK, DN_DK), lambda b, c: (0, 0, 0)),
            pl.BlockSpec((TAIL16, w), lambda b, c: (0, 0)),
            pl.BlockSpec((TAIL16, w), lambda b, c: (0, 1)),
            pl.BlockSpec((TAIL16, w), lambda b, c: (0, 2)),
        ],
        out_specs=[
            pl.BlockSpec((rb, w), lambda b, c: (b * ncb + c, 0)),
            pl.BlockSpec((1, DN_HEADS, DN_DK, DN_DK), lambda b, c: (b, 0, 0, 0)),
        ],
        out_shape=[
            jax.ShapeDtypeStruct((rows, w), BF16),
            jax.ShapeDtypeStruct((batch, DN_HEADS, DN_DK, DN_DK), F32),
        ],
        scratch_shapes=[
            pltpu.VMEM((DN_HEADS, DN_DK, DN_DK), F32),
            pltpu.VMEM((3, TAIL16, w), BF16),
        ],
        compiler_params=_cparams(2),
        name="gdn_scan",
    )(proj, proj, proj, proj, small, conv_w, conv_w, conv_w, alog_row, dtb_row, norm_w, s0,
      tail, tail, tail)


def _ssd_kernel(x_ref, b_ref, c_ref, z_ref, sm_ref, wx_ref, wb_ref, wc_ref, bx_ref, bb_ref, bc_ref,
                alog_ref, dtb_ref, d_ref, nw_ref, st0_ref, tx_ref, tb_ref, tc_ref,
                y_ref, stfin_ref, st_sc, tx_sc, tb_sc, tc_sc, *, cps, n_pad):
    c = pl.program_id(1)
    rb = cps * CHUNK

    @pl.when(c == 0)
    def _():
        st_sc[...] = st0_ref[...]
        tx_sc[...] = tx_ref[...]
        tb_sc[...] = tb_ref[...]
        tc_sc[...] = tc_ref[...]

    shift = _shift_matrix(4)

    def conv_silu(x_ref, w_ref, bias_ref, t_sc):
        x16 = x_ref[...]
        y = _causal_conv_bf16(x16, t_sc[...], w_ref[...], shift) + bias_ref[...]
        t_sc[...] = x16[rb - TAIL16:rb]
        return _silu(y)

    def conv_silu_f32(x_ref, w_ref, bias_ref, t_sc):
        x = x_ref[...].astype(F32)
        y = _causal_conv_val(x, t_sc[...], w_ref[...], 4) + bias_ref[...]
        t_sc[...] = x[rb - TAIL:rb]
        return _silu(y)

    xs = conv_silu(x_ref, wx_ref, bx_ref, tx_sc)
    bm = conv_silu_f32(b_ref, wb_ref, bb_ref, tb_sc)
    cm = conv_silu_f32(c_ref, wc_ref, bc_ref, tc_sc)

    sm = sm_ref[...]
    dt_all = _softplus(sm + dtb_ref[...])
    a_all = dt_all * (-jnp.exp(alog_ref[...]))
    d_row = d_ref[...]
    nw = nw_ref[...]

    lane = lax.broadcasted_iota(jnp.int32, (CHUNK, LANES), 1)
    rowi = lax.broadcasted_iota(jnp.int32, (CHUNK, LANES), 0)
    lo = lane < M2_P
    causal2 = rowi >= jnp.where(lo, lane, lane - M2_P)
    lo2 = lax.broadcasted_iota(jnp.int32, (2 * CHUNK, LANES), 1) < M2_P
    top2 = lax.broadcasted_iota(jnp.int32, (2 * CHUNK, LANES), 0) < CHUNK
    blk2 = lo2 == top2

    npair = M2_HEADS // 2
    ppg = npair // M2_GROUPS

    for cc in range(cps):
        r = slice(cc * CHUNK, (cc + 1) * CHUNK)
        acs = _cumsum_rows(a_all[r])
        zt = jnp.concatenate([acs, pltpu.roll(acs, LANES - 1, axis=1)], axis=0).T
        dt = dt_all[r]
        y_tiles = []
        for g in range(M2_GROUPS):
            gs = slice(g * M2_STATE, (g + 1) * M2_STATE)
            b16 = bm[r, gs].astype(BF16)
            c16 = cm[r, gs].astype(BF16)
            cbcb = _dot_nt(c16, jnp.concatenate([b16, b16], axis=0))
            st = st_sc[g]
            yoff = _dot(c16, st.astype(BF16))
            xdec_tiles = []
            ea_last = []
            for mm in range(ppg):
                m = g * ppg + mm
                la = SM_DT + 2 * m
                ea = jnp.where(lo, acs[:, la:la + 1], acs[:, la + 1:la + 2])
                edt = jnp.where(lo, dt[:, la:la + 1], dt[:, la + 1:la + 2])
                xs_m = xs[r, m * LANES:(m + 1) * LANES]
                xdt = xs_m * edt
                if n_pad:
                    xdt = jnp.where(rowi + cc * CHUNK >= n_pad, xdt, 0.0)
                rowp = zt[la:la + 1, :]
                lmat = jnp.exp(jnp.where(causal2, ea - rowp, -jnp.inf))
                wp = (cbcb * lmat).astype(BF16)
                x16 = xdt.astype(BF16)
                x2 = jnp.concatenate([x16, x16], axis=0)
                rhs = jnp.where(blk2, x2, jnp.zeros_like(x2))
                y = _dot(wp, rhs) + yoff[:, mm * LANES:(mm + 1) * LANES] * jnp.exp(ea)
                y_tiles.append(y + d_row[:, m * LANES:(m + 1) * LANES] * xs_m)
                eal = ea[CHUNK - 1:CHUNK]
                ea_last.append(eal)
                xdec_tiles.append((xdt * jnp.exp(eal - ea)).astype(BF16))
            xdec = jnp.concatenate(xdec_tiles, axis=1)
            sdec = jnp.exp(jnp.concatenate(ea_last, axis=1))
            bt16 = bm[r, gs].T.astype(BF16)
            st_sc[g] = st * sdec + _dot(bt16, xdec)
        for g in range(M2_GROUPS):
            tiles = []
            ssq = None
            for mm in range(ppg):
                m = g * ppg + mm
                z = z_ref[r, m * LANES:(m + 1) * LANES].astype(F32)
                t = y_tiles[m] * _silu(z)
                tiles.append(t)
                part = jnp.sum(t * t, axis=-1, keepdims=True)
                ssq = part if ssq is None else ssq + part
            scale = lax.rsqrt(ssq * (1.0 / M2_GW) + NORM_EPS)
            for mm in range(ppg):
                m = g * ppg + mm
                cs = slice(m * LANES, (m + 1) * LANES)
                y_ref[r, cs] = (tiles[mm] * scale * nw[:, cs]).astype(y_ref.dtype)

    @pl.when(c == pl.num_programs(1) - 1)
    def _():
        stfin_ref[0] = st_sc[...]


def ssd_scan(proj, small, conv_w, conv_b, alog_row, dtb_row, d_row, norm_w, st0, tail_x, tail_bc, *,
             batch, cps, n_pad):
    rows = proj.shape[0]
    rb = cps * CHUNK
    ncb = rows // batch // rb
    wx = M2_HEADS * M2_P
    wg = M2_GROUPS * M2_STATE

    def rowblk(width, off):
        return pl.BlockSpec((rb, width), lambda b, c: (b * ncb + c, off // width))

    def const(shape, *idx):
        return pl.BlockSpec(shape, lambda b, c: idx)

    kern = functools.partial(_ssd_kernel, cps=cps, n_pad=n_pad)
    return pl.pallas_call(
        kern,
        grid=(batch, ncb),
        in_specs=[
            rowblk(wx, COL_XS),
            rowblk(wg, COL_B),
            rowblk(wg, COL_C),
            rowblk(wx, COL_M2Z),
            pl.BlockSpec((rb, LANES), lambda b, c: (b * ncb + c, 0)),
            const((4, wx), 0, 0),
            const((4, wg), 0, wx // wg),
            const((4, wg), 0, wx // wg + 1),
            const((1, wx), 0, 0),
            const((1, wg), 0, wx // wg),
            const((1, wg), 0, wx // wg + 1),
            const((1, LANES), 0, 0),
            const((1, LANES), 0, 0),
            const((1, wx), 0, 0),
            const((1, wx), 0, 0),
            const((M2_GROUPS, M2_STATE, M2_GW), 0, 0, 0),
            const((TAIL16, wx), 0, 0),
            const((TAIL, wg), 0, 0),
            const((TAIL, wg), 0, 1),
        ],
        out_specs=[
            pl.BlockSpec((rb, wx), lambda b, c: (b * ncb + c, 0)),
            pl.BlockSpec((1, M2_GROUPS, M2_STATE, M2_GW), lambda b, c: (b, 0, 0, 0)),
        ],
        out_shape=[
            jax.ShapeDtypeStruct((rows, wx), BF16),
            jax.ShapeDtypeStruct((batch, M2_GROUPS, M2_STATE, M2_GW), F32),
        ],
        scratch_shapes=[
            pltpu.VMEM((M2_GROUPS, M2_STATE, M2_GW), F32),
            pltpu.VMEM((TAIL16, wx), BF16),
            pltpu.VMEM((TAIL, wg), F32),
            pltpu.VMEM((TAIL, wg), F32),
        ],
        compiler_params=_cparams(2),
        name="ssd_scan",
    )(proj, proj, proj, proj, small, conv_w, conv_w, conv_w, conv_b, conv_b, conv_b,
      alog_row, dtb_row, d_row, norm_w, st0, tail_x, tail_bc, tail_bc)


def _out_proj_kernel(x_ref, o_ref, y_ref, wo_ref, wy_ref, h_ref):
    h_ref[...] = x_ref[...] + _dot(o_ref[...], wo_ref[...]) + _dot(y_ref[...], wy_ref[...])


def out_proj(x, o, y, w_out, *, tm, tn):
    rows, d = x.shape
    km = o.shape[1]
    return pl.pallas_call(
        _out_proj_kernel,
        grid=(d // tn, rows // tm),
        in_specs=[
            pl.BlockSpec((tm, tn), lambda j, i: (i, j)),
            pl.BlockSpec((tm, km), lambda j, i: (i, 0)),
            pl.BlockSpec((tm, km), lambda j, i: (i, 0)),
            pl.BlockSpec((km, tn), lambda j, i: (0, j)),
            pl.BlockSpec((km, tn), lambda j, i: (1, j)),
        ],
        out_specs=pl.BlockSpec((tm, tn), lambda j, i: (i, j)),
        out_shape=jax.ShapeDtypeStruct((rows, d), F32),
        compiler_params=_cparams(2),
        name="out_proj",
    )(x, o, y, w_out, w_out)


def _conv_ffn_kernel(h_ref, gn_ref, wg_ref, wv_ref, cwg_ref, cwv_ref, wd_ref, tg_ref, tv_ref,
                     fn_ref, o_ref, hn_sc, acc_sc, ug_sc, uv_sc, cg_sc, cv_sc, *,
                     tiles_per_batch, n_split):
    i = pl.program_id(0)
    f = pl.program_id(1)
    nf = pl.num_programs(1)
    tm = h_ref.shape[0]
    hm = tm // n_split

    @pl.when(f == 0)
    def _():
        h = h_ref[...]
        ms = jnp.mean(h * h, axis=-1, keepdims=True)
        hn_sc[...] = (h * lax.rsqrt(ms + NORM_EPS) * gn_ref[...]).astype(BF16)
        acc_sc[...] = jnp.zeros_like(acc_sc)

    first = (i % tiles_per_batch) == 0
    ug_sc[0:TAIL] = jnp.where(first, tg_ref[...], cg_sc[f])
    uv_sc[0:TAIL] = jnp.where(first, tv_ref[...], cv_sc[f])

    for s in range(n_split):
        rows = slice(s * hm, (s + 1) * hm)
        urows = slice(TAIL + s * hm, TAIL + (s + 1) * hm)
        hn = hn_sc[rows]
        ug_sc[urows] = _dot(hn, wg_ref[...])
        uv_sc[urows] = _dot(hn, wv_ref[...])
    cg_sc[f] = ug_sc[tm:tm + TAIL]
    cv_sc[f] = uv_sc[tm:tm + TAIL]
    for s in range(n_split):
        rows = slice(s * hm, (s + 1) * hm)
        gate = _causal_conv(ug_sc, TAIL + s * hm, hm, cwg_ref[...], 3)
        val = _causal_conv(uv_sc, TAIL + s * hm, hm, cwv_ref[...], 3)
        act = (_silu(gate) * val).astype(BF16)
        acc_sc[rows] += _dot(act, wd_ref[...])

    @pl.when(f == nf - 1)
    def _():
        h2 = h_ref[...] + acc_sc[...]
        ms = jnp.mean(h2 * h2, axis=-1, keepdims=True)
        o_ref[...] = h2 * lax.rsqrt(ms + NORM_EPS) * fn_ref[...]


def conv_ffn(h, gain, w_up, conv_w, w_down, u_tail, final_gain, *, tm, tf, tiles_per_batch,
             n_split):
    rows, d = h.shape
    dff = w_down.shape[0]
    nf = dff // tf
    kern = functools.partial(_conv_ffn_kernel, tiles_per_batch=tiles_per_batch, n_split=n_split)
    return pl.pallas_call(
        kern,
        grid=(rows // tm, nf),
        in_specs=[
            pl.BlockSpec((tm, d), lambda i, f: (i, 0)),
            pl.BlockSpec((1, d), lambda i, f: (0, 0)),
            pl.BlockSpec((d, tf), lambda i, f: (0, f)),
            pl.BlockSpec((d, tf), lambda i, f: (0, nf + f)),
            pl.BlockSpec((3, tf), lambda i, f: (0, f)),
            pl.BlockSpec((3, tf), lambda i, f: (0, nf + f)),
            pl.BlockSpec((tf, d), lambda i, f: (f, 0)),
            pl.BlockSpec((TAIL, tf), lambda i, f: (0, f)),
            pl.BlockSpec((TAIL, tf), lambda i, f: (0, nf + f)),
            pl.BlockSpec((1, d), lambda i, f: (0, 0)),
        ],
        out_specs=pl.BlockSpec((tm, d), lambda i, f: (i, 0)),
        out_shape=jax.ShapeDtypeStruct((rows, d), F32),
        scratch_shapes=[
            pltpu.VMEM((tm, d), BF16),
            pltpu.VMEM((tm, d), F32),
            pltpu.VMEM((TAIL + tm, tf), F32),
            pltpu.VMEM((TAIL + tm, tf), F32),
            pltpu.VMEM((nf, TAIL, tf), F32),
            pltpu.VMEM((nf, TAIL, tf), F32),
        ],
        compiler_params=_cparams(2),
        name="conv_ffn",
    )(h, gain, w_up, w_up, conv_w, conv_w, w_down, u_tail, u_tail, final_gain)


def _row_tile(rows, pref):
    t = min(rows, pref)
    while rows % t:
        t //= 2
    return t


def _lane_row(vec, offset):
    return jnp.zeros((1, LANES), F32).at[0, offset:offset + vec.shape[0]].set(vec.astype(F32))


def kernel(x, meta_tokens, norm_mix_w, w_in, dn_conv_w, dn_a_log, dn_dt_bias, dn_norm_w, m2_conv_w, m2_conv_b, m2_a_log, m2_dt_bias, m2_d, m2_norm_w, w_out, norm_ffn_w, ffn_up, ffn_conv_w, ffn_down, norm_final_w):
    batch, seq, d = x.shape
    depth = w_in.shape[0]
    assert depth == 1 and d == D_MODEL and seq % CHUNK == 0
    rows = batch * seq
    pad = CHUNK - N_META

    w = w_in[0]
    o_qkv, o_z, o_b, o_a = 0, 6144, 8192, 8208
    o_m2z, o_xbc, o_dt = 8224, 10272, 13344
    w_big = jnp.concatenate(
        [w[:, o_qkv:o_z], w[:, o_z:o_b], w[:, o_m2z:o_xbc], w[:, o_xbc:o_dt]], axis=1).astype(BF16)
    w_small = jnp.concatenate(
        [w[:, o_b:o_a], w[:, o_a:o_m2z], w[:, o_dt:], jnp.zeros((d, LANES - 64), w.dtype)],
        axis=1).astype(BF16)
    w_out16 = w_out[0].astype(BF16)
    ffn_up16 = ffn_up[0].astype(BF16)
    ffn_down16 = ffn_down[0].astype(BF16)
    zero_small = jnp.zeros((d, LANES), BF16)

    gain_mix = norm_mix_w[0].reshape(1, d).astype(F32)
    gain_ffn = norm_ffn_w[0].reshape(1, d).astype(F32)
    gain_fin = norm_final_w.reshape(1, d).astype(F32)
    dn_alog_row = _lane_row(dn_a_log[0], SM_A)
    dn_dtb_row = _lane_row(dn_dt_bias[0], SM_A)
    m2_alog_row = _lane_row(m2_a_log[0], SM_DT)
    m2_dtb_row = _lane_row(m2_dt_bias[0], SM_DT)
    dn_cw = dn_conv_w[0].astype(F32)
    dn_nw = dn_norm_w[0].reshape(1, DN_DK).astype(F32)
    m2_cw = m2_conv_w[0].astype(F32)
    m2_cb = m2_conv_b[0].reshape(1, -1).astype(F32)
    m2_d_row = jnp.repeat(m2_d[0].astype(F32), M2_P).reshape(1, M2_HEADS * M2_P)
    m2_nw = m2_norm_w[0].reshape(1, -1).astype(F32)
    ffn_cw = ffn_conv_w[0].astype(F32)

    hm = jnp.concatenate([jnp.zeros((pad, d), F32), meta_tokens.astype(F32)], axis=0)
    pm_big, pm_small = norm_matmul(hm, gain_mix, w_big, w_small, tm=CHUNK, tn=1024, out_dtype=BF16)
    zeros_tail = jnp.zeros((TAIL16, N_BIG), BF16)
    o_m, s_meta = gdn_scan(pm_big, pm_small, dn_cw, dn_alog_row, dn_dtb_row, dn_nw,
                           jnp.zeros((DN_HEADS, DN_DK, DN_DK), F32), zeros_tail[:, :3 * 2048],
                           batch=1, cps=1)
    y_m, st_meta = ssd_scan(pm_big, pm_small, m2_cw, m2_cb, m2_alog_row, m2_dtb_row, m2_d_row, m2_nw,
                            jnp.zeros((M2_GROUPS, M2_STATE, M2_GW), F32),
                            zeros_tail[:, :2048], jnp.zeros((TAIL, N_BIG - COL_B), F32),
                            batch=1, cps=1, n_pad=pad)
    h1_m = out_proj(hm, o_m, y_m, w_out16, tm=CHUNK, tn=1024)
    u_m, _ = norm_matmul(h1_m, gain_ffn, ffn_up16, zero_small, tm=CHUNK, tn=1024, out_dtype=F32)
    tail_big = pm_big[CHUNK - TAIL16:]
    tail_qkv = tail_big[:, COL_Q:COL_DNZ]
    tail_xs = tail_big[:, COL_XS:COL_B]
    tail_bc = tail_big[TAIL16 - TAIL:, COL_B:N_BIG].astype(F32)
    tail_u = u_m[CHUNK - TAIL:]

    x2 = x.reshape(rows, d).astype(F32)
    p_big, p_small = norm_matmul(x2, gain_mix, w_big, w_small, tm=_row_tile(rows, 512), tn=3328,
                                 out_dtype=BF16)
    o_g, _ = gdn_scan(p_big, p_small, dn_cw, dn_alog_row, dn_dtb_row, dn_nw, s_meta[0], tail_qkv,
                      batch=batch, cps=GDN_CPS)
    y_g, _ = ssd_scan(p_big, p_small, m2_cw, m2_cb, m2_alog_row, m2_dtb_row, m2_d_row, m2_nw,
                      st_meta[0], tail_xs, tail_bc, batch=batch, cps=SSD_CPS, n_pad=0)
    h1 = out_proj(x2, o_g, y_g, w_out16, tm=_row_tile(rows, 512), tn=1024)
    tm_ffn = _row_tile(seq, 512)
    out = conv_ffn(h1, gain_ffn, ffn_up16, ffn_cw, ffn_down16, tail_u, gain_fin,
                   tm=tm_ffn, tf=512, tiles_per_batch=seq // tm_ffn, n_split=2)
    return out.reshape(batch, seq, d)
```

```python
import functools

import jax
import jax.numpy as jnp
from jax import lax
from jax.experimental import pallas as pl
from jax.experimental.pallas import tpu as pltpu

F32 = jnp.float32
BF16 = jnp.bfloat16

NORM_EPS = 1e-6
N_META = 16
CHUNK = 64
LANES = 128
TAIL = 8
TAIL16 = 16

DN_HEADS = 16
DN_DK = 128
M2_HEADS = 32
M2_P = 64
M2_GROUPS = 4
M2_STATE = 128
M2_GW = M2_HEADS * M2_P // M2_GROUPS

D_MODEL = 2048
COL_Q, COL_K, COL_V = 0, 2048, 4096
COL_DNZ = 6144
COL_M2Z = 8192
COL_XS = 10240
COL_B = 12288
COL_C = 12800
N_BIG = 13312
SM_BETA, SM_A, SM_DT = 0, 16, 32

VMEM_LIMIT = 56 * 1024 * 1024

GDN_CPS = 2
SSD_CPS = 2
FFN_SPLIT = (1, 1)
GDN_GROUPS = 2

def _cparams(n_axes):
    return pltpu.CompilerParams(dimension_semantics=("arbitrary",) * n_axes,
                                vmem_limit_bytes=VMEM_LIMIT)


def _dot(a, b):
    return jnp.dot(a, b, preferred_element_type=F32)


def _dot_nt(a, b):
    return lax.dot_general(a, b, (((1,), (1,)), ((), ())), preferred_element_type=F32)


def _bdot(a, b):
    return lax.dot_general(a, b, (((2,), (1,)), ((0,), (0,))), preferred_element_type=F32)


def _bdot_nt(a, b):
    return lax.dot_general(a, b, (((2,), (2,)), ((0,), (0,))), preferred_element_type=F32)


def _sigmoid(x):
    return 1.0 / (1.0 + jnp.exp(-x))


def _silu(x):
    return x * _sigmoid(x)


def _softplus(x):
    return jnp.maximum(x, 0.0) + jnp.log1p(jnp.exp(-jnp.abs(x)))


def _cumsum_rows(x):
    n = x.shape[0]
    row = lax.broadcasted_iota(jnp.int32, x.shape, 0)
    s = 1
    while s < n:
        x = x + jnp.where(row >= s, pltpu.roll(x, s, axis=0), 0.0)
        s *= 2
    return x


def _causal_conv(xe_ref, start, n, w, k):
    acc = None
    for j in range(k):
        term = xe_ref[pl.ds(start - (k - 1) + j, n), :] * w[j:j + 1]
        acc = term if acc is None else acc + term
    return acc


def _shift_matrix(k):
    m = TAIL16 + CHUNK
    r = lax.broadcasted_iota(jnp.int32, (CHUNK, (k - 1) * m), 0)
    c = lax.broadcasted_iota(jnp.int32, (CHUNK, (k - 1) * m), 1)
    hit = None
    for j in range(k - 1):
        sel = c == r + (j * m + TAIL16 - (k - 1) + j)
        hit = sel if hit is None else hit | sel
    return hit.astype(BF16)


def _causal_conv_bf16(x16, tail16, w, shift):
    n = x16.shape[0]
    k = w.shape[0]
    w16 = w.astype(BF16)
    outs = []
    for s in range(n // CHUNK):
        hist = tail16 if s == 0 else x16[s * CHUNK - TAIL16:s * CHUNK]
        xc = x16[s * CHUNK:(s + 1) * CHUNK]
        xe = jnp.concatenate([hist, xc], axis=0)
        taps = jnp.concatenate([xe * w16[j:j + 1] for j in range(k - 1)], axis=0)
        outs.append(_dot(shift, taps) + xc.astype(F32) * w[k - 1:k])
    return outs[0] if len(outs) == 1 else jnp.concatenate(outs, axis=0)


def _norm_matmul_kernel(x_ref, g_ref, w_ref, ws_ref, o_ref, os_ref, xn_ref):
    @pl.when(pl.program_id(1) == 0)
    def _():
        x = x_ref[...]
        ms = jnp.mean(x * x, axis=-1, keepdims=True)
        xn = (x * lax.rsqrt(ms + NORM_EPS) * g_ref[...]).astype(BF16)
        xn_ref[...] = xn
        os_ref[...] = _dot(xn, ws_ref[...])

    o_ref[...] = _dot(xn_ref[...], w_ref[...]).astype(o_ref.dtype)


def norm_matmul(x, gain, w, w_small, *, tm, tn, out_dtype):
    rows, k = x.shape
    n = w.shape[1]
    ns = w_small.shape[1]
    return pl.pallas_call(
        _norm_matmul_kernel,
        grid=(rows // tm, n // tn),
        in_specs=[
            pl.BlockSpec((tm, k), lambda i, j: (i, 0)),
            pl.BlockSpec((1, k), lambda i, j: (0, 0)),
            pl.BlockSpec((k, tn), lambda i, j: (0, j)),
            pl.BlockSpec((k, ns), lambda i, j: (0, 0)),
        ],
        out_specs=[
            pl.BlockSpec((tm, tn), lambda i, j: (i, j)),
            pl.BlockSpec((tm, ns), lambda i, j: (i, 0)),
        ],
        out_shape=[
            jax.ShapeDtypeStruct((rows, n), out_dtype),
            jax.ShapeDtypeStruct((rows, ns), F32),
        ],
        scratch_shapes=[pltpu.VMEM((tm, k), BF16)],
        compiler_params=_cparams(2),
        name="norm_matmul",
    )(x, gain, w, w_small)


def _gdn_kernel(q_ref, k_ref, v_ref, z_ref, sm_ref, wq_ref, wk_ref, wv_ref, alog_ref, dtb_ref,
                nw_ref, s0_ref, tq_ref, tk_ref, tv_ref,
                o_ref, sfin_ref, s_sc, tail_sc, *, cps):
    c = pl.program_id(1)
    rb = cps * CHUNK

    @pl.when(c == 0)
    def _():
        s_sc[...] = s0_ref[...]
        tail_sc[0] = tq_ref[...]
        tail_sc[1] = tk_ref[...]
        tail_sc[2] = tv_ref[...]

    shift = _shift_matrix(4)

    def conv_steps(hs, act):
        cs = slice(hs[0] * DN_DK, (hs[-1] + 1) * DN_DK)
        for idx, (x_ref, w_ref) in enumerate(((q_ref, wq_ref), (k_ref, wk_ref), (v_ref, wv_ref))):
            x16 = x_ref[:, cs]
            y = _causal_conv_bf16(x16, tail_sc[idx, :, cs], w_ref[:, cs], shift)
            tail_sc[idx, :, cs] = x16[rb - TAIL16:rb]
            yield
            act.append(_silu(y))

    sm = sm_ref[...]
    beta_all = _sigmoid(sm)
    g_all = -jnp.exp(alog_ref[...]) * _softplus(sm + dtb_ref[...])
    nw = nw_ref[...]

    ri = lax.broadcasted_iota(jnp.int32, (CHUNK, CHUNK), 0)
    ci = lax.broadcasted_iota(jnp.int32, (CHUNK, CHUNK), 1)
    causal = ri >= ci
    strict = ri > ci
    eye = (ri == ci).astype(F32)

    def rows(cc):
        return slice(cc * CHUNK, (cc + 1) * CHUNK)

    def prepare(cc, hs, act):
        def heads(a):
            return jnp.stack([a[rows(cc), i * DN_DK:(i + 1) * DN_DK] for i in range(len(hs))],
                             axis=0)

        def cols(a, base):
            return jnp.stack([a[:, base + h:base + h + 1] for h in hs], axis=0)

        gc = _cumsum_rows(g_all[rows(cc)])
        gct = jnp.concatenate([gc, gc], axis=0).T
        beta_c = beta_all[rows(cc)]
        eg_c = jnp.exp(gc)
        kdec_c = jnp.exp(gc[CHUNK - 1:CHUNK] - gc)
        beg_c = beta_c * pltpu.roll(eg_c, LANES - SM_A, axis=1)
        q = heads(act[0])
        k = heads(act[1])
        v = heads(act[2])
        grow = jnp.stack([gct[SM_A + h:SM_A + h + 1, :CHUNK] for h in hs], axis=0)
        q = q * (lax.rsqrt(jnp.sum(q * q, axis=-1, keepdims=True) + NORM_EPS) * DN_DK ** -0.5)
        k = k * lax.rsqrt(jnp.sum(k * k, axis=-1, keepdims=True) + NORM_EPS)
        kd = k * cols(kdec_c, SM_A)
        return dict(
            beta=cols(beta_c, SM_BETA),
            decay=jnp.exp(jnp.where(causal[None], cols(gc, SM_A) - grow, -jnp.inf)),
            slast=cols(eg_c[CHUNK - 1:CHUNK], SM_A),
            qk16=jnp.concatenate([q.astype(BF16), k.astype(BF16)], axis=1),
            rhs=jnp.concatenate([v * cols(beta_c, SM_BETA), k * cols(beg_c, SM_BETA)],
                                axis=2).astype(BF16),
            qe16=(q * cols(eg_c, SM_A)).astype(BF16),
            kdt16=jnp.stack([kd[i].T for i in range(len(hs))], axis=0).astype(BF16),
        )

    def chunk_steps(cc, hs, state, act):
        pr = prepare(cc, hs, act)
        gram = _bdot_nt(pr["qk16"], pr["qk16"][:, CHUNK:])
        yield
        qkm16 = (gram[:, :CHUNK] * pr["decay"]).astype(BF16)
        a_mat = jnp.where(strict[None], pr["beta"] * gram[:, CHUNK:] * pr["decay"], 0.0)
        x_inv = eye[None] - a_mat
        a16 = a_mat.astype(BF16)
        p = _bdot(a16, a16)
        yield
        for lvl in range(5):
            p16 = p.astype(BF16)
            if lvl < 4:
                xp = _bdot(jnp.concatenate([x_inv.astype(BF16), p16], axis=1), p16)
                yield
                x_inv = x_inv + xp[:, :CHUNK]
                p = xp[:, CHUNK:]
            else:
                last = _bdot(x_inv.astype(BF16), p16)
                yield
                x_inv = x_inv + last
        wu = _bdot(x_inv.astype(BF16), pr["rhs"])
        yield
        u = wu[:, :, :DN_DK]
        w16 = wu[:, :, DN_DK:].astype(BF16)
        assert state["done"] == cc
        s = state["s"]
        s16 = s.astype(BF16)
        ws = _bdot(w16, s16)
        yield
        vn16 = (u - ws).astype(BF16)
        o = _bdot(pr["qe16"], s16) + _bdot(qkm16, vn16)
        state["s"] = s * pr["slast"] + _bdot(pr["kdt16"], vn16)
        state["done"] = cc + 1
        yield
        z = jnp.stack([z_ref[rows(cc), h * DN_DK:(h + 1) * DN_DK] for h in hs], axis=0).astype(F32)
        on = o * lax.rsqrt(jnp.mean(o * o, axis=-1, keepdims=True) + NORM_EPS) * nw * _silu(z)
        on = on.astype(o_ref.dtype)
        for i, h in enumerate(hs):
            o_ref[rows(cc), h * DN_DK:(h + 1) * DN_DK] = on[i]

    gsz = DN_HEADS // GDN_GROUPS
    groups = [list(range(g * gsz, (g + 1) * gsz)) for g in range(GDN_GROUPS)]
    states = [dict(s=s_sc[hs[0]:hs[-1] + 1], done=0) for hs in groups]
    def group_steps(hs, state):
        act = []
        yield from conv_steps(hs, act)
        for cc in range(cps):
            yield from chunk_steps(cc, hs, state, act)

    active = [group_steps(hs, st) for hs, st in zip(groups, states)]
    while active:
        for gen in list(active):
            try:
                next(gen)
            except StopIteration:
                active.remove(gen)
    for hs, st in zip(groups, states):
        s_sc[hs[0]:hs[-1] + 1] = st["s"]

    @pl.when(c == pl.num_programs(1) - 1)
    def _():
        sfin_ref[0] = s_sc[...]


def gdn_scan(proj, small, conv_w, alog_row, dtb_row, norm_w, s0, tail, *, batch, cps):
    rows = proj.shape[0]
    rb = cps * CHUNK
    ncb = rows // batch // rb
    w = DN_HEADS * DN_DK

    def col(off):
        return lambda b, c: (b * ncb + c, off // w)

    kern = functools.partial(_gdn_kernel, cps=cps)
    return pl.pallas_call(
        kern,
        grid=(batch, ncb),
        in_specs=[
            pl.BlockSpec((rb, w), col(COL_Q)),
            pl.BlockSpec((rb, w), col(COL_K)),
            pl.BlockSpec((rb, w), col(COL_V)),
            pl.BlockSpec((rb, w), col(COL_DNZ)),
            pl.BlockSpec((rb, LANES), lambda b, c: (b * ncb + c, 0)),
            pl.BlockSpec((4, w), lambda b, c: (0, 0)),
            pl.BlockSpec((4, w), lambda b, c: (0, 1)),
            pl.BlockSpec((4, w), lambda b, c: (0, 2)),
            pl.BlockSpec((1, LANES), lambda b, c: (0, 0)),
            pl.BlockSpec((1, LANES), lambda b, c: (0, 0)),
            pl.BlockSpec((1, DN_DK), lambda b, c: (0, 0)),
            pl.BlockSpec((DN_HEADS, DN_DK, DN_DK), lambda b, c: (0, 0, 0)),
            pl.BlockSpec((TAIL16, w), lambda b, c: (0, 0)),
            pl.BlockSpec((TAIL16, w), lambda b, c: (0, 1)),
            pl.BlockSpec((TAIL16, w), lambda b, c: (0, 2)),
        ],
        out_specs=[
            pl.BlockSpec((rb, w), lambda b, c: (b * ncb + c, 0)),
            pl.BlockSpec((1, DN_HEADS, DN_DK, DN_DK), lambda b, c: (b, 0, 0, 0)),
        ],
        out_shape=[
            jax.ShapeDtypeStruct((rows, w), BF16),
            jax.ShapeDtypeStruct((batch, DN_HEADS, DN_DK, DN_DK), F32),
        ],
        scratch_shapes=[
            pltpu.VMEM((DN_HEADS, DN_DK, DN_DK), F32),
            pltpu.VMEM((3, TAIL16, w), BF16),
        ],
        compiler_params=_cparams(2),
        name="gdn_scan",
    )(proj, proj, proj, proj, small, conv_w, conv_w, conv_w, alog_row, dtb_row, norm_w, s0,
      tail, tail, tail)


def _ssd_kernel(x_ref, bc_ref, z_ref, sm_ref, wx_ref, wbc_ref, bx_ref, bbc_ref,
                alog_ref, dtb_ref, d_ref, nw_ref, st0_ref, tx_ref, tbc_ref,
                y_ref, stfin_ref, st_sc, tx_sc, tbc_sc, *, cps, n_pad):
    c = pl.program_id(1)
    rb = cps * CHUNK

    @pl.when(c == 0)
    def _():
        st_sc[...] = st0_ref[...]
        tx_sc[...] = tx_ref[...]
        tbc_sc[...] = tbc_ref[...]

    shift = _shift_matrix(4)

    def conv_silu(x_ref, w_ref, bias_ref, t_sc):
        x16 = x_ref[...]
        y = _causal_conv_bf16(x16, t_sc[...], w_ref[...], shift) + bias_ref[...]
        t_sc[...] = x16[rb - TAIL16:rb]
        return _silu(y)

    xs = conv_silu(x_ref, wx_ref, bx_ref, tx_sc)
    bcm = conv_silu(bc_ref, wbc_ref, bbc_ref, tbc_sc)
    wg = M2_GROUPS * M2_STATE
    bm = bcm[:, :wg]
    cm = bcm[:, wg:]

    sm = sm_ref[...]
    dt_all = _softplus(sm + dtb_ref[...])
    a_all = dt_all * (-jnp.exp(alog_ref[...]))
    d_row = d_ref[...]
    nw = nw_ref[...]

    lane = lax.broadcasted_iota(jnp.int32, (CHUNK, LANES), 1)
    rowi = lax.broadcasted_iota(jnp.int32, (CHUNK, LANES), 0)
    lo = lane < M2_P
    causal2 = rowi >= jnp.where(lo, lane, lane - M2_P)
    lo2 = lax.broadcasted_iota(jnp.int32, (2 * CHUNK, LANES), 1) < M2_P
    top2 = lax.broadcasted_iota(jnp.int32, (2 * CHUNK, LANES), 0) < CHUNK
    blk2 = lo2 == top2

    npair = M2_HEADS // 2
    ppg = npair // M2_GROUPS

    for cc in range(cps):
        r = slice(cc * CHUNK, (cc + 1) * CHUNK)
        acs = _cumsum_rows(a_all[r])
        zt = jnp.concatenate([acs, pltpu.roll(acs, LANES - 1, axis=1)], axis=0).T
        dt = dt_all[r]
        y_tiles = []
        for g in range(M2_GROUPS):
            gs = slice(g * M2_STATE, (g + 1) * M2_STATE)
            b16 = bm[r, gs].astype(BF16)
            c16 = cm[r, gs].astype(BF16)
            cbcb = _dot_nt(c16, jnp.concatenate([b16, b16], axis=0))
            st = st_sc[g]
            yoff = _dot(c16, st.astype(BF16))
            xdec_tiles = []
            ea_last = []
            for mm in range(ppg):
                m = g * ppg + mm
                la = SM_DT + 2 * m
                ea = jnp.where(lo, acs[:, la:la + 1], acs[:, la + 1:la + 2])
                edt = jnp.where(lo, dt[:, la:la + 1], dt[:, la + 1:la + 2])
                xs_m = xs[r, m * LANES:(m + 1) * LANES]
                xdt = xs_m * edt
                if n_pad:
                    xdt = jnp.where(rowi + cc * CHUNK >= n_pad, xdt, 0.0)
                rowp = zt[la:la + 1, :]
                lmat = jnp.exp(jnp.where(causal2, ea - rowp, -jnp.inf))
                wp = (cbcb * lmat).astype(BF16)
                x16 = xdt.astype(BF16)
                x2 = jnp.concatenate([x16, x16], axis=0)
                rhs = jnp.where(blk2, x2, jnp.zeros_like(x2))
                y = _dot(wp, rhs) + yoff[:, mm * LANES:(mm + 1) * LANES] * jnp.exp(ea)
                y_tiles.append(y + d_row[:, m * LANES:(m + 1) * LANES] * xs_m)
                eal = ea[CHUNK - 1:CHUNK]
                ea_last.append(eal)
                xdec_tiles.append((xdt * jnp.exp(eal - ea)).astype(BF16))
            xdec = jnp.concatenate(xdec_tiles, axis=1)
            sdec = jnp.exp(jnp.concatenate(ea_last, axis=1))
            bt16 = bm[r, gs].T.astype(BF16)
            st_sc[g] = st * sdec + _dot(bt16, xdec)
        for g in range(M2_GROUPS):
            tiles = []
            ssq = None
            for mm in range(ppg):
                m = g * ppg + mm
                z = z_ref[r, m * LANES:(m + 1) * LANES].astype(F32)
                t = y_tiles[m] * _silu(z)
                tiles.append(t)
                part = jnp.sum(t * t, axis=-1, keepdims=True)
                ssq = part if ssq is None else ssq + part
            scale = lax.rsqrt(ssq * (1.0 / M2_GW) + NORM_EPS)
            for mm in range(ppg):
                m = g * ppg + mm
                cs = slice(m * LANES, (m + 1) * LANES)
                y_ref[r, cs] = (tiles[mm] * scale * nw[:, cs]).astype(y_ref.dtype)

    @pl.when(c == pl.num_programs(1) - 1)
    def _():
        stfin_ref[0] = st_sc[...]


def ssd_scan(proj, small, conv_w, conv_b, alog_row, dtb_row, d_row, norm_w, st0, tail_x, tail_bc, *,
             batch, cps, n_pad):
    rows = proj.shape[0]
    rb = cps * CHUNK
    ncb = rows // batch // rb
    wx = M2_HEADS * M2_P
    wg = M2_GROUPS * M2_STATE

    def rowblk(width, off):
        return pl.BlockSpec((rb, width), lambda b, c: (b * ncb + c, off // width))

    def const(shape, *idx):
        return pl.BlockSpec(shape, lambda b, c: idx)

    kern = functools.partial(_ssd_kernel, cps=cps, n_pad=n_pad)
    return pl.pallas_call(
        kern,
        grid=(batch, ncb),
        in_specs=[
            rowblk(wx, COL_XS),
            rowblk(2 * wg, COL_B),
            rowblk(wx, COL_M2Z),
            pl.BlockSpec((rb, LANES), lambda b, c: (b * ncb + c, 0)),
            const((4, wx), 0, 0),
            const((4, 2 * wg), 0, 2),
            const((1, wx), 0, 0),
            const((1, 2 * wg), 0, 2),
            const((1, LANES), 0, 0),
            const((1, LANES), 0, 0),
            const((1, wx), 0, 0),
            const((1, wx), 0, 0),
            const((M2_GROUPS, M2_STATE, M2_GW), 0, 0, 0),
            const((TAIL16, wx), 0, 0),
            const((TAIL16, 2 * wg), 0, 0),
        ],
        out_specs=[
            pl.BlockSpec((rb, wx), lambda b, c: (b * ncb + c, 0)),
            pl.BlockSpec((1, M2_GROUPS, M2_STATE, M2_GW), lambda b, c: (b, 0, 0, 0)),
        ],
        out_shape=[
            jax.ShapeDtypeStruct((rows, wx), BF16),
            jax.ShapeDtypeStruct((batch, M2_GROUPS, M2_STATE, M2_GW), F32),
        ],
        scratch_shapes=[
            pltpu.VMEM((M2_GROUPS, M2_STATE, M2_GW), F32),
            pltpu.VMEM((TAIL16, wx), BF16),
            pltpu.VMEM((TAIL16, 2 * wg), BF16),
        ],
        compiler_params=_cparams(2),
        name="ssd_scan",
    )(proj, proj, proj, small, conv_w, conv_w, conv_b, conv_b,
      alog_row, dtb_row, d_row, norm_w, st0, tail_x, tail_bc)


def _out_proj_kernel(x_ref, o_ref, y_ref, wo_ref, wy_ref, h_ref):
    h_ref[...] = x_ref[...] + _dot(o_ref[...], wo_ref[...]) + _dot(y_ref[...], wy_ref[...])


def out_proj(x, o, y, w_out, *, tm, tn):
    rows, d = x.shape
    km = o.shape[1]
    return pl.pallas_call(
        _out_proj_kernel,
        grid=(d // tn, rows // tm),
        in_specs=[
            pl.BlockSpec((tm, tn), lambda j, i: (i, j)),
            pl.BlockSpec((tm, km), lambda j, i: (i, 0)),
            pl.BlockSpec((tm, km), lambda j, i: (i, 0)),
            pl.BlockSpec((km, tn), lambda j, i: (0, j)),
            pl.BlockSpec((km, tn), lambda j, i: (1, j)),
        ],
        out_specs=pl.BlockSpec((tm, tn), lambda j, i: (i, j)),
        out_shape=jax.ShapeDtypeStruct((rows, d), F32),
        compiler_params=_cparams(2),
        name="out_proj",
    )(x, o, y, w_out, w_out)


def _conv_ffn_kernel(h_ref, gn_ref, wg_ref, wv_ref, cwg_ref, cwv_ref, wd_ref, tg_ref, tv_ref,
                     fn_ref, o_ref, hn_sc, acc_sc, ug_sc, uv_sc, cg_sc, cv_sc, *,
                     tiles_per_batch, n_split):
    i = pl.program_id(0)
    f = pl.program_id(1)
    nf = pl.num_programs(1)
    tm = h_ref.shape[0]
    bounds = [0]
    for frac in n_split:
        bounds.append(bounds[-1] + tm * frac // sum(n_split))
    subs = [slice(a, b) for a, b in zip(bounds[:-1], bounds[1:])]

    @pl.when(f == 0)
    def _():
        h = h_ref[...]
        ms = jnp.mean(h * h, axis=-1, keepdims=True)
        hn_sc[...] = (h * lax.rsqrt(ms + NORM_EPS) * gn_ref[...]).astype(BF16)
        acc_sc[...] = jnp.zeros_like(acc_sc)

    first = (i % tiles_per_batch) == 0
    ug_sc[0:TAIL] = jnp.where(first, tg_ref[...], cg_sc[f])
    uv_sc[0:TAIL] = jnp.where(first, tv_ref[...], cv_sc[f])

    for rows in subs:
        urows = slice(TAIL + rows.start, TAIL + rows.stop)
        hn = hn_sc[rows]
        ug_sc[urows] = _dot(hn, wg_ref[...])
        uv_sc[urows] = _dot(hn, wv_ref[...])
    cg_sc[f] = ug_sc[tm:tm + TAIL]
    cv_sc[f] = uv_sc[tm:tm + TAIL]
    for rows in subs:
        hm = rows.stop - rows.start
        gate = _causal_conv(ug_sc, TAIL + rows.start, hm, cwg_ref[...], 3)
        val = _causal_conv(uv_sc, TAIL + rows.start, hm, cwv_ref[...], 3)
        act = (_silu(gate) * val).astype(BF16)
        acc_sc[rows] += _dot(act, wd_ref[...])

    @pl.when(f == nf - 1)
    def _():
        h2 = h_ref[...] + acc_sc[...]
        ms = jnp.mean(h2 * h2, axis=-1, keepdims=True)
        o_ref[...] = h2 * lax.rsqrt(ms + NORM_EPS) * fn_ref[...]


def conv_ffn(h, gain, w_up, conv_w, w_down, u_tail, final_gain, *, tm, tf, tiles_per_batch,
             n_split):
    rows, d = h.shape
    dff = w_down.shape[0]
    nf = dff // tf
    kern = functools.partial(_conv_ffn_kernel, tiles_per_batch=tiles_per_batch, n_split=n_split)
    return pl.pallas_call(
        kern,
        grid=(rows // tm, nf),
        in_specs=[
            pl.BlockSpec((tm, d), lambda i, f: (i, 0)),
            pl.BlockSpec((1, d), lambda i, f: (0, 0)),
            pl.BlockSpec((d, tf), lambda i, f: (0, f)),
            pl.BlockSpec((d, tf), lambda i, f: (0, nf + f)),
            pl.BlockSpec((3, tf), lambda i, f: (0, f)),
            pl.BlockSpec((3, tf), lambda i, f: (0, nf + f)),
            pl.BlockSpec((tf, d), lambda i, f: (f, 0)),
            pl.BlockSpec((TAIL, tf), lambda i, f: (0, f)),
            pl.BlockSpec((TAIL, tf), lambda i, f: (0, nf + f)),
            pl.BlockSpec((1, d), lambda i, f: (0, 0)),
        ],
        out_specs=pl.BlockSpec((tm, d), lambda i, f: (i, 0)),
        out_shape=jax.ShapeDtypeStruct((rows, d), F32),
        scratch_shapes=[
            pltpu.VMEM((tm, d), BF16),
            pltpu.VMEM((tm, d), F32),
            pltpu.VMEM((TAIL + tm, tf), F32),
            pltpu.VMEM((TAIL + tm, tf), F32),
            pltpu.VMEM((nf, TAIL, tf), F32),
            pltpu.VMEM((nf, TAIL, tf), F32),
        ],
        compiler_params=_cparams(2),
        name="conv_ffn",
    )(h, gain, w_up, w_up, conv_w, conv_w, w_down, u_tail, u_tail, final_gain)


def _row_tile(rows, pref):
    t = min(rows, pref)
    while rows % t:
        t //= 2
    return t


def _lane_row(vec, offset):
    return jnp.zeros((1, LANES), F32).at[0, offset:offset + vec.shape[0]].set(vec.astype(F32))


def kernel(x, meta_tokens, norm_mix_w, w_in, dn_conv_w, dn_a_log, dn_dt_bias, dn_norm_w, m2_conv_w, m2_conv_b, m2_a_log, m2_dt_bias, m2_d, m2_norm_w, w_out, norm_ffn_w, ffn_up, ffn_conv_w, ffn_down, norm_final_w):
    batch, seq, d = x.shape
    depth = w_in.shape[0]
    assert depth == 1 and d == D_MODEL and seq % CHUNK == 0
    rows = batch * seq
    pad = CHUNK - N_META

    w = w_in[0]
    o_qkv, o_z, o_b, o_a = 0, 6144, 8192, 8208
    o_m2z, o_xbc, o_dt = 8224, 10272, 13344
    w_big = jnp.concatenate(
        [w[:, o_qkv:o_z], w[:, o_z:o_b], w[:, o_m2z:o_xbc], w[:, o_xbc:o_dt]], axis=1).astype(BF16)
    w_small = jnp.concatenate(
        [w[:, o_b:o_a], w[:, o_a:o_m2z], w[:, o_dt:], jnp.zeros((d, LANES - 64), w.dtype)],
        axis=1).astype(BF16)
    w_out16 = w_out[0].astype(BF16)
    ffn_up16 = ffn_up[0].astype(BF16)
    ffn_down16 = ffn_down[0].astype(BF16)
    zero_small = jnp.zeros((d, LANES), BF16)

    gain_mix = norm_mix_w[0].reshape(1, d).astype(F32)
    gain_ffn = norm_ffn_w[0].reshape(1, d).astype(F32)
    gain_fin = norm_final_w.reshape(1, d).astype(F32)
    dn_alog_row = _lane_row(dn_a_log[0], SM_A)
    dn_dtb_row = _lane_row(dn_dt_bias[0], SM_A)
    m2_alog_row = _lane_row(m2_a_log[0], SM_DT)
    m2_dtb_row = _lane_row(m2_dt_bias[0], SM_DT)
    dn_cw = dn_conv_w[0].astype(F32)
    dn_nw = dn_norm_w[0].reshape(1, DN_DK).astype(F32)
    m2_cw = m2_conv_w[0].astype(F32)
    m2_cb = m2_conv_b[0].reshape(1, -1).astype(F32)
    m2_d_row = jnp.repeat(m2_d[0].astype(F32), M2_P).reshape(1, M2_HEADS * M2_P)
    m2_nw = m2_norm_w[0].reshape(1, -1).astype(F32)
    ffn_cw = ffn_conv_w[0].astype(F32)

    hm = jnp.concatenate([jnp.zeros((pad, d), F32), meta_tokens.astype(F32)], axis=0)
    pm_big, pm_small = norm_matmul(hm, gain_mix, w_big, w_small, tm=CHUNK, tn=1024, out_dtype=BF16)
    zeros_tail = jnp.zeros((TAIL16, N_BIG), BF16)
    o_m, s_meta = gdn_scan(pm_big, pm_small, dn_cw, dn_alog_row, dn_dtb_row, dn_nw,
                           jnp.zeros((DN_HEADS, DN_DK, DN_DK), F32), zeros_tail[:, :3 * 2048],
                           batch=1, cps=1)
    y_m, st_meta = ssd_scan(pm_big, pm_small, m2_cw, m2_cb, m2_alog_row, m2_dtb_row, m2_d_row, m2_nw,
                            jnp.zeros((M2_GROUPS, M2_STATE, M2_GW), F32),
                            zeros_tail[:, :2048], zeros_tail[:, :N_BIG - COL_B],
                            batch=1, cps=1, n_pad=pad)
    h1_m = out_proj(hm, o_m, y_m, w_out16, tm=CHUNK, tn=1024)
    u_m, _ = norm_matmul(h1_m, gain_ffn, ffn_up16, zero_small, tm=CHUNK, tn=1024, out_dtype=F32)
    tail_big = pm_big[CHUNK - TAIL16:]
    tail_qkv = tail_big[:, COL_Q:COL_DNZ]
    tail_xs = tail_big[:, COL_XS:COL_B]
    tail_bc = tail_big[:, COL_B:N_BIG]
    tail_u = u_m[CHUNK - TAIL:]

    x2 = x.reshape(rows, d).astype(F32)
    p_big, p_small = norm_matmul(x2, gain_mix, w_big, w_small, tm=_row_tile(rows, 512), tn=3328,
                                 out_dtype=BF16)
    o_g, _ = gdn_scan(p_big, p_small, dn_cw, dn_alog_row, dn_dtb_row, dn_nw, s_meta[0], tail_qkv,
                      batch=batch, cps=GDN_CPS)
    y_g, _ = ssd_scan(p_big, p_small, m2_cw, m2_cb, m2_alog_row, m2_dtb_row, m2_d_row, m2_nw,
                      st_meta[0], tail_xs, tail_bc, batch=batch, cps=SSD_CPS, n_pad=0)
    h1 = out_proj(x2, o_g, y_g, w_out16, tm=_row_tile(rows, 512), tn=1024)
    tm_ffn = _row_tile(seq, 512)
    out = conv_ffn(h1, gain_ffn, ffn_up16, ffn_cw, ffn_down16, tail_u, gain_fin,
                   tm=tm_ffn, tf=512, tiles_per_batch=seq // tm_ffn, n_split=FFN_SPLIT)
    return out.reshape(batch, seq, d)
```

```python
import functools

import jax
import jax.numpy as jnp
from jax import lax
from jax.experimental import pallas as pl
from jax.experimental.pallas import tpu as pltpu

F32 = jnp.float32
BF16 = jnp.bfloat16

NORM_EPS = 1e-6
N_META = 16
CHUNK = 64
LANES = 128
TAIL = 8
TAIL16 = 16

DN_HEADS = 16
DN_DK = 128
M2_HEADS = 32
M2_P = 64
M2_GROUPS = 4
M2_STATE = 128
M2_GW = M2_HEADS * M2_P // M2_GROUPS

D_MODEL = 2048
COL_Q, COL_K, COL_V = 0, 2048, 4096
COL_DNZ = 6144
COL_M2Z = 8192
COL_XS = 10240
COL_B = 12288
COL_C = 12800
N_BIG = 13312
SM_BETA, SM_A, SM_DT = 0, 16, 32

VMEM_LIMIT = 56 * 1024 * 1024

GDN_CPS = 8
SSD_CPS = 2
GDN_GROUPS = 2

def _cparams(n_axes):
    return pltpu.CompilerParams(dimension_semantics=("arbitrary",) * n_axes,
                                vmem_limit_bytes=VMEM_LIMIT)


def _dot(a, b):
    return jnp.dot(a, b, preferred_element_type=F32)


def _dot_nt(a, b):
    return lax.dot_general(a, b, (((1,), (1,)), ((), ())), preferred_element_type=F32)


def _bdot(a, b):
    return lax.dot_general(a, b, (((2,), (1,)), ((0,), (0,))), preferred_element_type=F32)


def _bdot_nt(a, b):
    return lax.dot_general(a, b, (((2,), (2,)), ((0,), (0,))), preferred_element_type=F32)


def _sigmoid(x):
    return 1.0 / (1.0 + jnp.exp(-x))


def _silu(x):
    return x * _sigmoid(x)


def _softplus(x):
    return jnp.maximum(x, 0.0) + jnp.log1p(jnp.exp(-jnp.abs(x)))


def _cumsum_rows(x):
    n = x.shape[0]
    row = lax.broadcasted_iota(jnp.int32, x.shape, 0)
    s = 1
    while s < n:
        x = x + jnp.where(row >= s, pltpu.roll(x, s, axis=0), 0.0)
        s *= 2
    return x


def _causal_conv(xe_ref, start, n, w, k):
    acc = None
    for j in range(k):
        term = xe_ref[pl.ds(start - (k - 1) + j, n), :] * w[j:j + 1]
        acc = term if acc is None else acc + term
    return acc


def _causal_conv_val(x, tail, w, k):
    xe = jnp.concatenate([tail, x], axis=0)
    acc = x * w[k - 1:k]
    for j in range(k - 1):
        acc = acc + pltpu.roll(xe, k - 1 - j, axis=0)[TAIL:] * w[j:j + 1]
    return acc


def _shift_matrix(k):
    m = TAIL16 + CHUNK
    r = lax.broadcasted_iota(jnp.int32, (CHUNK, (k - 1) * m), 0)
    c = lax.broadcasted_iota(jnp.int32, (CHUNK, (k - 1) * m), 1)
    hit = None
    for j in range(k - 1):
        sel = c == r + (j * m + TAIL16 - (k - 1) + j)
        hit = sel if hit is None else hit | sel
    return hit.astype(BF16)


def _causal_conv_bf16(x16, tail16, w, shift):
    n = x16.shape[0]
    k = w.shape[0]
    w16 = w.astype(BF16)
    outs = []
    for s in range(n // CHUNK):
        hist = tail16 if s == 0 else x16[s * CHUNK - TAIL16:s * CHUNK]
        xc = x16[s * CHUNK:(s + 1) * CHUNK]
        xe = jnp.concatenate([hist, xc], axis=0)
        taps = jnp.concatenate([xe * w16[j:j + 1] for j in range(k - 1)], axis=0)
        outs.append(_dot(shift, taps) + xc.astype(F32) * w[k - 1:k])
    return outs[0] if len(outs) == 1 else jnp.concatenate(outs, axis=0)


def _norm_matmul_kernel(x_ref, g_ref, w_ref, ws_ref, o_ref, os_ref, xn_ref):
    @pl.when(pl.program_id(1) == 0)
    def _():
        x = x_ref[...]
        ms = jnp.mean(x * x, axis=-1, keepdims=True)
        xn = (x * lax.rsqrt(ms + NORM_EPS) * g_ref[...]).astype(BF16)
        xn_ref[...] = xn
        os_ref[...] = _dot(xn, ws_ref[...])

    o_ref[...] = _dot(xn_ref[...], w_ref[...]).astype(o_ref.dtype)


def norm_matmul(x, gain, w, w_small, *, tm, tn, out_dtype):
    rows, k = x.shape
    n = w.shape[1]
    ns = w_small.shape[1]
    return pl.pallas_call(
        _norm_matmul_kernel,
        grid=(rows // tm, n // tn),
        in_specs=[
            pl.BlockSpec((tm, k), lambda i, j: (i, 0)),
            pl.BlockSpec((1, k), lambda i, j: (0, 0)),
            pl.BlockSpec((k, tn), lambda i, j: (0, j)),
            pl.BlockSpec((k, ns), lambda i, j: (0, 0)),
        ],
        out_specs=[
            pl.BlockSpec((tm, tn), lambda i, j: (i, j)),
            pl.BlockSpec((tm, ns), lambda i, j: (i, 0)),
        ],
        out_shape=[
            jax.ShapeDtypeStruct((rows, n), out_dtype),
            jax.ShapeDtypeStruct((rows, ns), F32),
        ],
        scratch_shapes=[pltpu.VMEM((tm, k), BF16)],
        compiler_params=_cparams(2),
        name="norm_matmul",
    )(x, gain, w, w_small)


def _gdn_kernel(q_ref, k_ref, v_ref, z_ref, sm_ref, wq_ref, wk_ref, wv_ref, alog_ref, dtb_ref,
                nw_ref, s0_ref, tq_ref, tk_ref, tv_ref,
                o_ref, sfin_ref, s_sc, tail_sc, *, cps):
    c = pl.program_id(1)
    rb = cps * CHUNK

    @pl.when(c == 0)
    def _():
        s_sc[...] = s0_ref[...]
        tail_sc[0] = tq_ref[...]
        tail_sc[1] = tk_ref[...]
        tail_sc[2] = tv_ref[...]

    shift = _shift_matrix(4)

    def conv_silu(x_ref, w_ref, idx):
        x16 = x_ref[...]
        y = _causal_conv_bf16(x16, tail_sc[idx], w_ref[...], shift)
        tail_sc[idx] = x16[rb - TAIL16:rb]
        return _silu(y)

    qs = conv_silu(q_ref, wq_ref, 0)
    ks = conv_silu(k_ref, wk_ref, 1)
    vs = conv_silu(v_ref, wv_ref, 2)

    sm = sm_ref[...]
    beta_all = _sigmoid(sm)
    g_all = -jnp.exp(alog_ref[...]) * _softplus(sm + dtb_ref[...])
    nw = nw_ref[...]

    ri = lax.broadcasted_iota(jnp.int32, (CHUNK, CHUNK), 0)
    ci = lax.broadcasted_iota(jnp.int32, (CHUNK, CHUNK), 1)
    causal = ri >= ci
    strict = ri > ci
    eye = (ri == ci).astype(F32)

    def rows(cc):
        return slice(cc * CHUNK, (cc + 1) * CHUNK)

    def prepare(cc, hs):
        def heads(a):
            return jnp.stack([a[rows(cc), h * DN_DK:(h + 1) * DN_DK] for h in hs], axis=0)

        def cols(a, base):
            return jnp.stack([a[:, base + h:base + h + 1] for h in hs], axis=0)

        gc = _cumsum_rows(g_all[rows(cc)])
        gct = jnp.concatenate([gc, gc], axis=0).T
        beta_c = beta_all[rows(cc)]
        eg_c = jnp.exp(gc)
        kdec_c = jnp.exp(gc[CHUNK - 1:CHUNK] - gc)
        beg_c = beta_c * pltpu.roll(eg_c, LANES - SM_A, axis=1)
        q = heads(qs)
        k = heads(ks)
        v = heads(vs)
        grow = jnp.stack([gct[SM_A + h:SM_A + h + 1, :CHUNK] for h in hs], axis=0)
        q = q * (lax.rsqrt(jnp.sum(q * q, axis=-1, keepdims=True) + NORM_EPS) * DN_DK ** -0.5)
        k = k * lax.rsqrt(jnp.sum(k * k, axis=-1, keepdims=True) + NORM_EPS)
        kd = k * cols(kdec_c, SM_A)
        return dict(
            beta=cols(beta_c, SM_BETA),
            decay=jnp.exp(jnp.where(causal[None], cols(gc, SM_A) - grow, -jnp.inf)),
            slast=cols(eg_c[CHUNK - 1:CHUNK], SM_A),
            qk16=jnp.concatenate([q.astype(BF16), k.astype(BF16)], axis=1),
            rhs=jnp.concatenate([v * cols(beta_c, SM_BETA), k * cols(beg_c, SM_BETA)],
                                axis=2).astype(BF16),
            qe16=(q * cols(eg_c, SM_A)).astype(BF16),
            kdt16=jnp.stack([kd[i].T for i in range(len(hs))], axis=0).astype(BF16),
        )

    def chunk_steps(cc, hs, state):
        pr = prepare(cc, hs)
        gram = _bdot_nt(pr["qk16"], pr["qk16"][:, CHUNK:])
        yield
        qkm16 = (gram[:, :CHUNK] * pr["decay"]).astype(BF16)
        a_mat = jnp.where(strict[None], pr["beta"] * gram[:, CHUNK:] * pr["decay"], 0.0)
        x_inv = eye[None] - a_mat
        a16 = a_mat.astype(BF16)
        p = _bdot(a16, a16)
        yield
        for lvl in range(5):
            p16 = p.astype(BF16)
            if lvl < 4:
                xp = _bdot(jnp.concatenate([x_inv.astype(BF16), p16], axis=1), p16)
                yield
                x_inv = x_inv + xp[:, :CHUNK]
                p = xp[:, CHUNK:]
            else:
                last = _bdot(x_inv.astype(BF16), p16)
                yield
                x_inv = x_inv + last
        wu = _bdot(x_inv.astype(BF16), pr["rhs"])
        yield
        u = wu[:, :, :DN_DK]
        w16 = wu[:, :, DN_DK:].astype(BF16)
        assert state["done"] == cc
        s = state["s"]
        s16 = s.astype(BF16)
        ws = _bdot(w16, s16)
        yield
        vn16 = (u - ws).astype(BF16)
        o = _bdot(pr["qe16"], s16) + _bdot(qkm16, vn16)
        state["s"] = s * pr["slast"] + _bdot(pr["kdt16"], vn16)
        state["done"] = cc + 1
        yield
        z = jnp.stack([z_ref[rows(cc), h * DN_DK:(h + 1) * DN_DK] for h in hs], axis=0).astype(F32)
        on = o * lax.rsqrt(jnp.mean(o * o, axis=-1, keepdims=True) + NORM_EPS) * nw * _silu(z)
        on = on.astype(o_ref.dtype)
        for i, h in enumerate(hs):
            o_ref[rows(cc), h * DN_DK:(h + 1) * DN_DK] = on[i]

    gsz = DN_HEADS // GDN_GROUPS
    groups = [list(range(g * gsz, (g + 1) * gsz)) for g in range(GDN_GROUPS)]
    states = [dict(s=s_sc[hs[0]:hs[-1] + 1], done=0) for hs in groups]
    for cc in range(cps):
        active = [chunk_steps(cc, hs, st) for hs, st in zip(groups, states)]
        while active:
            for gen in list(active):
                try:
                    next(gen)
                except StopIteration:
                    active.remove(gen)
    for hs, st in zip(groups, states):
        s_sc[hs[0]:hs[-1] + 1] = st["s"]

    @pl.when(c == pl.num_programs(1) - 1)
    def _():
        sfin_ref[0] = s_sc[...]


def gdn_scan(proj, small, conv_w, alog_row, dtb_row, norm_w, s0, tail, *, batch, cps):
    rows = proj.shape[0]
    rb = cps * CHUNK
    ncb = rows // batch // rb
    w = DN_HEADS * DN_DK

    def col(off):
        return lambda b, c: (b * ncb + c, off // w)

    kern = functools.partial(_gdn_kernel, cps=cps)
    return pl.pallas_call(
        kern,
        grid=(batch, ncb),
        in_specs=[
            pl.BlockSpec((rb, w), col(COL_Q)),
            pl.BlockSpec((rb, w), col(COL_K)),
            pl.BlockSpec((rb, w), col(COL_V)),
            pl.BlockSpec((rb, w), col(COL_DNZ)),
            pl.BlockSpec((rb, LANES), lambda b, c: (b * ncb + c, 0)),
            pl.BlockSpec((4, w), lambda b, c: (0, 0)),
            pl.BlockSpec((4, w), lambda b, c: (0, 1)),
            pl.BlockSpec((4, w), lambda b, c: (0, 2)),
            pl.BlockSpec((1, LANES), lambda b, c: (0, 0)),
            pl.BlockSpec((1, LANES), lambda b, c: (0, 0)),
            pl.BlockSpec((1, DN_DK), lambda b, c: (0, 0)),
            pl.BlockSpec((DN_HEADS, DN_DK, DN_DK), lambda b, c: (0, 0, 0)),
            pl.BlockSpec((TAIL16, w), lambda b, c: (0, 0)),
            pl.BlockSpec((TAIL16, w), lambda b, c: (0, 1)),
            pl.BlockSpec((TAIL16, w), lambda b, c: (0, 2)),
        ],
        out_specs=[
            pl.BlockSpec((rb, w), lambda b, c: (b * ncb + c, 0)),
            pl.BlockSpec((1, DN_HEADS, DN_DK, DN_DK), lambda b, c: (b, 0, 0, 0)),
        ],
        out_shape=[
            jax.ShapeDtypeStruct((rows, w), BF16),
            jax.ShapeDtypeStruct((batch, DN_HEADS, DN_DK, DN_DK), F32),
        ],
        scratch_shapes=[
            pltpu.VMEM((DN_HEADS, DN_DK, DN_DK), F32),
            pltpu.VMEM((3, TAIL16, w), BF16),
        ],
        compiler_params=_cparams(2),
        name="gdn_scan",
    )(proj, proj, proj, proj, small, conv_w, conv_w, conv_w, alog_row, dtb_row, norm_w, s0,
      tail, tail, tail)


def _ssd_kernel(x_ref, b_ref, c_ref, z_ref, sm_ref, wx_ref, wb_ref, wc_ref, bx_ref, bb_ref, bc_ref,
                alog_ref, dtb_ref, d_ref, nw_ref, st0_ref, tx_ref, tb_ref, tc_ref,
                y_ref, stfin_ref, st_sc, tx_sc, tb_sc, tc_sc, *, cps, n_pad):
    c = pl.program_id(1)
    rb = cps * CHUNK

    @pl.when(c == 0)
    def _():
        st_sc[...] = st0_ref[...]
        tx_sc[...] = tx_ref[...]
        tb_sc[...] = tb_ref[...]
        tc_sc[...] = tc_ref[...]

    shift = _shift_matrix(4)

    def conv_silu(x_ref, w_ref, bias_ref, t_sc):
        x16 = x_ref[...]
        y = _causal_conv_bf16(x16, t_sc[...], w_ref[...], shift) + bias_ref[...]
        t_sc[...] = x16[rb - TAIL16:rb]
        return _silu(y)

    def conv_silu_f32(x_ref, w_ref, bias_ref, t_sc):
        x = x_ref[...].astype(F32)
        y = _causal_conv_val(x, t_sc[...], w_ref[...], 4) + bias_ref[...]
        t_sc[...] = x[rb - TAIL:rb]
        return _silu(y)

    xs = conv_silu(x_ref, wx_ref, bx_ref, tx_sc)
    bm = conv_silu_f32(b_ref, wb_ref, bb_ref, tb_sc)
    cm = conv_silu_f32(c_ref, wc_ref, bc_ref, tc_sc)

    sm = sm_ref[...]
    dt_all = _softplus(sm + dtb_ref[...])
    a_all = dt_all * (-jnp.exp(alog_ref[...]))
    d_row = d_ref[...]
    nw = nw_ref[...]

    lane = lax.broadcasted_iota(jnp.int32, (CHUNK, LANES), 1)
    rowi = lax.broadcasted_iota(jnp.int32, (CHUNK, LANES), 0)
    lo = lane < M2_P
    causal2 = rowi >= jnp.where(lo, lane, lane - M2_P)
    lo2 = lax.broadcasted_iota(jnp.int32, (2 * CHUNK, LANES), 1) < M2_P
    top2 = lax.broadcasted_iota(jnp.int32, (2 * CHUNK, LANES), 0) < CHUNK
    blk2 = lo2 == top2

    npair = M2_HEADS // 2
    ppg = npair // M2_GROUPS

    for cc in range(cps):
        r = slice(cc * CHUNK, (cc + 1) * CHUNK)
        acs = _cumsum_rows(a_all[r])
        zt = jnp.concatenate([acs, pltpu.roll(acs, LANES - 1, axis=1)], axis=0).T
        dt = dt_all[r]
        y_tiles = []
        for g in range(M2_GROUPS):
            gs = slice(g * M2_STATE, (g + 1) * M2_STATE)
            b16 = bm[r, gs].astype(BF16)
            c16 = cm[r, gs].astype(BF16)
            cbcb = _dot_nt(c16, jnp.concatenate([b16, b16], axis=0))
            st = st_sc[g]
            yoff = _dot(c16, st.astype(BF16))
            xdec_tiles = []
            ea_last = []
            for mm in range(ppg):
                m = g * ppg + mm
                la = SM_DT + 2 * m
                ea = jnp.where(lo, acs[:, la:la + 1], acs[:, la + 1:la + 2])
                edt = jnp.where(lo, dt[:, la:la + 1], dt[:, la + 1:la + 2])
                xs_m = xs[r, m * LANES:(m + 1) * LANES]
                xdt = xs_m * edt
                if n_pad:
                    xdt = jnp.where(rowi + cc * CHUNK >= n_pad, xdt, 0.0)
                rowp = zt[la:la + 1, :]
                lmat = jnp.exp(jnp.where(causal2, ea - rowp, -jnp.inf))
                wp = (cbcb * lmat).astype(BF16)
                x16 = xdt.astype(BF16)
                x2 = jnp.concatenate([x16, x16], axis=0)
                rhs = jnp.where(blk2, x2, jnp.zeros_like(x2))
                y = _dot(wp, rhs) + yoff[:, mm * LANES:(mm + 1) * LANES] * jnp.exp(ea)
                y_tiles.append(y + d_row[:, m * LANES:(m + 1) * LANES] * xs_m)
                eal = ea[CHUNK - 1:CHUNK]
                ea_last.append(eal)
                xdec_tiles.append((xdt * jnp.exp(eal - ea)).astype(BF16))
            xdec = jnp.concatenate(xdec_tiles, axis=1)
            sdec = jnp.exp(jnp.concatenate(ea_last, axis=1))
            bt16 = bm[r, gs].T.astype(BF16)
            st_sc[g] = st * sdec + _dot(bt16, xdec)
        for g in range(M2_GROUPS):
            tiles = []
            ssq = None
            for mm in range(ppg):
                m = g * ppg + mm
                z = z_ref[r, m * LANES:(m + 1) * LANES].astype(F32)
                t = y_tiles[m] * _silu(z)
                tiles.append(t)
                part = jnp.sum(t * t, axis=-1, keepdims=True)
                ssq = part if ssq is None else ssq + part
            scale = lax.rsqrt(ssq * (1.0 / M2_GW) + NORM_EPS)
            for mm in range(ppg):
                m = g * ppg + mm
                cs = slice(m * LANES, (m + 1) * LANES)
                y_ref[r, cs] = (tiles[mm] * scale * nw[:, cs]).astype(y_ref.dtype)

    @pl.when(c == pl.num_programs(1) - 1)
    def _():
        stfin_ref[0] = st_sc[...]


def ssd_scan(proj, small, conv_w, conv_b, alog_row, dtb_row, d_row, norm_w, st0, tail_x, tail_bc, *,
             batch, cps, n_pad):
    rows = proj.shape[0]
    rb = cps * CHUNK
    ncb = rows // batch // rb
    wx = M2_HEADS * M2_P
    wg = M2_GROUPS * M2_STATE

    def rowblk(width, off):
        return pl.BlockSpec((rb, width), lambda b, c: (b * ncb + c, off // width))

    def const(shape, *idx):
        return pl.BlockSpec(shape, lambda b, c: idx)

    kern = functools.partial(_ssd_kernel, cps=cps, n_pad=n_pad)
    return pl.pallas_call(
        kern,
        grid=(batch, ncb),
        in_specs=[
            rowblk(wx, COL_XS),
            rowblk(wg, COL_B),
            rowblk(wg, COL_C),
            rowblk(wx, COL_M2Z),
            pl.BlockSpec((rb, LANES), lambda b, c: (b * ncb + c, 0)),
            const((4, wx), 0, 0),
            const((4, wg), 0, wx // wg),
            const((4, wg), 0, wx // wg + 1),
            const((1, wx), 0, 0),
            const((1, wg), 0, wx // wg),
            const((1, wg), 0, wx // wg + 1),
            const((1, LANES), 0, 0),
            const((1, LANES), 0, 0),
            const((1, wx), 0, 0),
            const((1, wx), 0, 0),
            const((M2_GROUPS, M2_STATE, M2_GW), 0, 0, 0),
            const((TAIL16, wx), 0, 0),
            const((TAIL, wg), 0, 0),
            const((TAIL, wg), 0, 1),
        ],
        out_specs=[
            pl.BlockSpec((rb, wx), lambda b, c: (b * ncb + c, 0)),
            pl.BlockSpec((1, M2_GROUPS, M2_STATE, M2_GW), lambda b, c: (b, 0, 0, 0)),
        ],
        out_shape=[
            jax.ShapeDtypeStruct((rows, wx), BF16),
            jax.ShapeDtypeStruct((batch, M2_GROUPS, M2_STATE, M2_GW), F32),
        ],
        scratch_shapes=[
            pltpu.VMEM((M2_GROUPS, M2_STATE, M2_GW), F32),
            pltpu.VMEM((TAIL16, wx), BF16),
            pltpu.VMEM((TAIL, wg), F32),
            pltpu.VMEM((TAIL, wg), F32),
        ],
        compiler_params=_cparams(2),
        name="ssd_scan",
    )(proj, proj, proj, proj, small, conv_w, conv_w, conv_w, conv_b, conv_b, conv_b,
      alog_row, dtb_row, d_row, norm_w, st0, tail_x, tail_bc, tail_bc)


def _out_proj_kernel(x_ref, o_ref, y_ref, wo_ref, wy_ref, h_ref):
    h_ref[...] = x_ref[...] + _dot(o_ref[...], wo_ref[...]) + _dot(y_ref[...], wy_ref[...])


def out_proj(x, o, y, w_out, *, tm, tn):
    rows, d = x.shape
    km = o.shape[1]
    return pl.pallas_call(
        _out_proj_kernel,
        grid=(d // tn, rows // tm),
        in_specs=[
            pl.BlockSpec((tm, tn), lambda j, i: (i, j)),
            pl.BlockSpec((tm, km), lambda j, i: (i, 0)),
            pl.BlockSpec((tm, km), lambda j, i: (i, 0)),
            pl.BlockSpec((km, tn), lambda j, i: (0, j)),
            pl.BlockSpec((km, tn), lambda j, i: (1, j)),
        ],
        out_specs=pl.BlockSpec((tm, tn), lambda j, i: (i, j)),
        out_shape=jax.ShapeDtypeStruct((rows, d), F32),
        compiler_params=_cparams(2),
        name="out_proj",
    )(x, o, y, w_out, w_out)


def _conv_ffn_kernel(h_ref, gn_ref, wg_ref, wv_ref, cwg_ref, cwv_ref, wd_ref, tg_ref, tv_ref,
                     fn_ref, o_ref, hn_sc, acc_sc, ug_sc, uv_sc, cg_sc, cv_sc, *,
                     tiles_per_batch, n_split):
    i = pl.program_id(0)
    f = pl.program_id(1)
    nf = pl.num_programs(1)
    tm = h_ref.shape[0]
    hm = tm // n_split

    @pl.when(f == 0)
    def _():
        h = h_ref[...]
        ms = jnp.mean(h * h, axis=-1, keepdims=True)
        hn_sc[...] = (h * lax.rsqrt(ms + NORM_EPS) * gn_ref[...]).astype(BF16)
        acc_sc[...] = jnp.zeros_like(acc_sc)

    first = (i % tiles_per_batch) == 0
    ug_sc[0:TAIL] = jnp.where(first, tg_ref[...], cg_sc[f])
    uv_sc[0:TAIL] = jnp.where(first, tv_ref[...], cv_sc[f])

    for s in range(n_split):
        rows = slice(s * hm, (s + 1) * hm)
        urows = slice(TAIL + s * hm, TAIL + (s + 1) * hm)
        hn = hn_sc[rows]
        ug_sc[urows] = _dot(hn, wg_ref[...])
        uv_sc[urows] = _dot(hn, wv_ref[...])
    cg_sc[f] = ug_sc[tm:tm + TAIL]
    cv_sc[f] = uv_sc[tm:tm + TAIL]
    for s in range(n_split):
        rows = slice(s * hm, (s + 1) * hm)
        gate = _causal_conv(ug_sc, TAIL + s * hm, hm, cwg_ref[...], 3)
        val = _causal_conv(uv_sc, TAIL + s * hm, hm, cwv_ref[...], 3)
        act = (_silu(gate) * val).astype(BF16)
        acc_sc[rows] += _dot(act, wd_ref[...])

    @pl.when(f == nf - 1)
    def _():
        h2 = h_ref[...] + acc_sc[...]
        ms = jnp.mean(h2 * h2, axis=-1, keepdims=True)
        o_ref[...] = h2 * lax.rsqrt(ms + NORM_EPS) * fn_ref[...]


def conv_ffn(h, gain, w_up, conv_w, w_down, u_tail, final_gain, *, tm, tf, tiles_per_batch,
             n_split):
    rows, d = h.shape
    dff = w_down.shape[0]
    nf = dff // tf
    kern = functools.partial(_conv_ffn_kernel, tiles_per_batch=tiles_per_batch, n_split=n_split)
    return pl.pallas_call(
        kern,
        grid=(rows // tm, nf),
        in_specs=[
            pl.BlockSpec((tm, d), lambda i, f: (i, 0)),
            pl.BlockSpec((1, d), lambda i, f: (0, 0)),
            pl.BlockSpec((d, tf), lambda i, f: (0, f)),
            pl.BlockSpec((d, tf), lambda i, f: (0, nf + f)),
            pl.BlockSpec((3, tf), lambda i, f: (0, f)),
            pl.BlockSpec((3, tf), lambda i, f: (0, nf + f)),
            pl.BlockSpec((tf, d), lambda i, f: (f, 0)),
            pl.BlockSpec((TAIL, tf), lambda i, f: (0, f)),
            pl.BlockSpec((TAIL, tf), lambda i, f: (0, nf + f)),
            pl.BlockSpec((1, d), lambda i, f: (0, 0)),
        ],
        out_specs=pl.BlockSpec((tm, d), lambda i, f: (i, 0)),
        out_shape=jax.ShapeDtypeStruct((rows, d), F32),
        scratch_shapes=[
            pltpu.VMEM((tm, d), BF16),
            pltpu.VMEM((tm, d), F32),
            pltpu.VMEM((TAIL + tm, tf), F32),
            pltpu.VMEM((TAIL + tm, tf), F32),
            pltpu.VMEM((nf, TAIL, tf), F32),
            pltpu.VMEM((nf, TAIL, tf), F32),
        ],
        compiler_params=_cparams(2),
        name="conv_ffn",
    )(h, gain, w_up, w_up, conv_w, conv_w, w_down, u_tail, u_tail, final_gain)


def _row_tile(rows, pref):
    t = min(rows, pref)
    while rows % t:
        t //= 2
    return t


def _lane_row(vec, offset):
    return jnp.zeros((1, LANES), F32).at[0, offset:offset + vec.shape[0]].set(vec.astype(F32))


def kernel(x, meta_tokens, norm_mix_w, w_in, dn_conv_w, dn_a_log, dn_dt_bias, dn_norm_w, m2_conv_w, m2_conv_b, m2_a_log, m2_dt_bias, m2_d, m2_norm_w, w_out, norm_ffn_w, ffn_up, ffn_conv_w, ffn_down, norm_final_w):
    batch, seq, d = x.shape
    depth = w_in.shape[0]
    assert depth == 1 and d == D_MODEL and seq % CHUNK == 0
    rows = batch * seq
    pad = CHUNK - N_META

    w = w_in[0]
    o_qkv, o_z, o_b, o_a = 0, 6144, 8192, 8208
    o_m2z, o_xbc, o_dt = 8224, 10272, 13344
    w_big = jnp.concatenate(
        [w[:, o_qkv:o_z], w[:, o_z:o_b], w[:, o_m2z:o_xbc], w[:, o_xbc:o_dt]], axis=1).astype(BF16)
    w_small = jnp.concatenate(
        [w[:, o_b:o_a], w[:, o_a:o_m2z], w[:, o_dt:], jnp.zeros((d, LANES - 64), w.dtype)],
        axis=1).astype(BF16)
    w_out16 = w_out[0].astype(BF16)
    ffn_up16 = ffn_up[0].astype(BF16)
    ffn_down16 = ffn_down[0].astype(BF16)
    zero_small = jnp.zeros((d, LANES), BF16)

    gain_mix = norm_mix_w[0].reshape(1, d).astype(F32)
    gain_ffn = norm_ffn_w[0].reshape(1, d).astype(F32)
    gain_fin = norm_final_w.reshape(1, d).astype(F32)
    dn_alog_row = _lane_row(dn_a_log[0], SM_A)
    dn_dtb_row = _lane_row(dn_dt_bias[0], SM_A)
    m2_alog_row = _lane_row(m2_a_log[0], SM_DT)
    m2_dtb_row = _lane_row(m2_dt_bias[0], SM_DT)
    dn_cw = dn_conv_w[0].astype(F32)
    dn_nw = dn_norm_w[0].reshape(1, DN_DK).astype(F32)
    m2_cw = m2_conv_w[0].astype(F32)
    m2_cb = m2_conv_b[0].reshape(1, -1).astype(F32)
    m2_d_row = jnp.repeat(m2_d[0].astype(F32), M2_P).reshape(1, M2_HEADS * M2_P)
    m2_nw = m2_norm_w[0].reshape(1, -1).astype(F32)
    ffn_cw = ffn_conv_w[0].astype(F32)

    hm = jnp.concatenate([jnp.zeros((pad, d), F32), meta_tokens.astype(F32)], axis=0)
    pm_big, pm_small = norm_matmul(hm, gain_mix, w_big, w_small, tm=CHUNK, tn=1024, out_dtype=BF16)
    zeros_tail = jnp.zeros((TAIL16, N_BIG), BF16)
    o_m, s_meta = gdn_scan(pm_big, pm_small, dn_cw, dn_alog_row, dn_dtb_row, dn_nw,
                           jnp.zeros((DN_HEADS, DN_DK, DN_DK), F32), zeros_tail[:, :3 * 2048],
                           batch=1, cps=1)
    y_m, st_meta = ssd_scan(pm_big, pm_small, m2_cw, m2_cb, m2_alog_row, m2_dtb_row, m2_d_row, m2_nw,
                            jnp.zeros((M2_GROUPS, M2_STATE, M2_GW), F32),
                            zeros_tail[:, :2048], jnp.zeros((TAIL, N_BIG - COL_B), F32),
                            batch=1, cps=1, n_pad=pad)
    h1_m = out_proj(hm, o_m, y_m, w_out16, tm=CHUNK, tn=1024)
    u_m, _ = norm_matmul(h1_m, gain_ffn, ffn_up16, zero_small, tm=CHUNK, tn=1024, out_dtype=F32)
    tail_big = pm_big[CHUNK - TAIL16:]
    tail_qkv = tail_big[:, COL_Q:COL_DNZ]
    tail_xs = tail_big[:, COL_XS:COL_B]
    tail_bc = tail_big[TAIL16 - TAIL:, COL_B:N_BIG].astype(F32)
    tail_u = u_m[CHUNK - TAIL:]

    x2 = x.reshape(rows, d).astype(F32)
    p_big, p_small = norm_matmul(x2, gain_mix, w_big, w_small, tm=_row_tile(rows, 512), tn=3328,
                                 out_dtype=BF16)
    o_g, _ = gdn_scan(p_big, p_small, dn_cw, dn_alog_row, dn_dtb_row, dn_nw, s_meta[0], tail_qkv,
                      batch=batch, cps=GDN_CPS)
    y_g, _ = ssd_scan(p_big, p_small, m2_cw, m2_cb, m2_alog_row, m2_dtb_row, m2_d_row, m2_nw,
                      st_meta[0], tail_xs, tail_bc, batch=batch, cps=SSD_CPS, n_pad=0)
    h1 = out_proj(x2, o_g, y_g, w_out16, tm=_row_tile(rows, 512), tn=1024)
    tm_ffn = _row_tile(seq, 512)
    out = conv_ffn(h1, gain_ffn, ffn_up16, ffn_cw, ffn_down16, tail_u, gain_fin,
                   tm=tm_ffn, tf=512, tiles_per_batch=seq // tm_ffn, n_split=2)
    return out.reshape(batch, seq, d)
```
